```python
import jax, jax.numpy as jnp
from jax import lax
import numpy as np

D_MODEL = 2048
BATCH = 1
SEQ = 8192
DEPTH = 2

P_DIM = 256
EPS = 1e-6
NEG = -1e30
BRANCH_WIDTH = 2048
N_BRANCH = 3

A_INNER = BRANCH_WIDTH
A_HEAD_DIM = 64
A_HEADS = A_INNER // A_HEAD_DIM
A_GROUPS = 8
A_STATE = 128
A_CONV = 4
A_CONV_CH = A_INNER + 2 * A_GROUPS * A_STATE
A_CHUNK = 128

B_HEADS = 16
B_KV_HEADS = 4
B_HEAD_DIM = BRANCH_WIDTH // B_HEADS
B_KV_DIM = B_KV_HEADS * B_HEAD_DIM
CMP_LEN = 32
CMP_STRIDE = 16
CMP_HIDDEN = 256
SEL_LEN = 64
SEL_TOPK = 16
SEL_LOCAL = 2
WINDOW = 512
Q_BLOCK = 128

C_HEADS = 8
C_HEAD_DIM = BRANCH_WIDTH // C_HEADS
C_CHUNK = 128
ROPE_BASE = 10000.0

IN_SIZES = (
    A_INNER, A_INNER, A_GROUPS * A_STATE, A_GROUPS * A_STATE, A_HEADS,
    BRANCH_WIDTH, B_KV_DIM, B_KV_DIM, B_KV_DIM, B_KV_DIM, B_KV_DIM, B_KV_DIM, 3 * B_HEADS, BRANCH_WIDTH,
    BRANCH_WIDTH, BRANCH_WIDTH, BRANCH_WIDTH, BRANCH_WIDTH,
    N_BRANCH * D_MODEL,
)
IN_DIM = sum(IN_SIZES)

kernel_name = 'hybrid_ssd_nsa_retention_block'


def rms_norm(x, gain=None):
    xf = x.astype(jnp.float32)
    y = xf * lax.rsqrt(jnp.mean(xf * xf, axis=-1, keepdims=True) + EPS)
    if gain is not None:
        y = y * gain.astype(jnp.float32)
    return y.astype(x.dtype)


def masked_softmax(s, mask):
    s = jnp.where(mask, s.astype(jnp.float32), NEG)
    return jax.nn.softmax(s, axis=-1) * mask


def causal_dwconv(u, w, b):
    k, c = w.shape
    y = lax.conv_general_dilated(u, w[:, None, :].astype(u.dtype), window_strides=(1,),
                                 padding=[(k - 1, 0)], dimension_numbers=('NWC', 'WIO', 'NWC'),
                                 feature_group_count=c)
    return y + b.astype(u.dtype)


def ssd_mixer(xs, z, bm, cm, dt, conv_w, conv_b, dt_bias, a_log, d_skip, norm_w):
    b, L, _ = xs.shape
    xbc = jax.nn.silu(causal_dwconv(jnp.concatenate([xs, bm, cm], axis=-1), conv_w, conv_b))
    xs, bm, cm = jnp.split(xbc, [A_INNER, A_INNER + A_GROUPS * A_STATE], axis=-1)
    dt = jax.nn.softplus((dt + dt_bias).astype(jnp.float32))
    a = -jnp.exp(a_log.astype(jnp.float32)) * dt
    nc, r = L // A_CHUNK, A_HEADS // A_GROUPS
    xh = xs.reshape(b, nc, A_CHUNK, A_GROUPS, r, A_HEAD_DIM)
    xdt = xh * dt.reshape(b, nc, A_CHUNK, A_GROUPS, r, 1).astype(xh.dtype)
    bm = bm.reshape(b, nc, A_CHUNK, A_GROUPS, A_STATE)
    cm = cm.reshape(b, nc, A_CHUNK, A_GROUPS, A_STATE)
    a_cs = jnp.cumsum(a.reshape(b, nc, A_CHUNK, A_GROUPS, r), axis=2)
    causal = jnp.tril(jnp.ones((A_CHUNK, A_CHUNK), dtype=bool))[None, None, :, :, None, None]
    seg = a_cs[:, :, :, None] - a_cs[:, :, None, :]
    decay = jnp.exp(jnp.where(causal, seg, -jnp.inf))
    cb = jnp.einsum('bclgn,bcsgn->bclsg', cm, bm)
    y_diag = jnp.einsum('bclsg,bclsgr,bcsgrp->bclgrp', cb, decay, xdt)
    decay_end = jnp.exp(a_cs[:, :, -1:] - a_cs)
    states = jnp.einsum('bclgn,bclgr,bclgrp->bcgrpn', bm, decay_end, xdt)
    chunk_decay = jnp.exp(a_cs[:, :, -1])

    def step(hs, inp):
        s_c, d_c = inp
        return hs * d_c[..., None, None] + s_c, hs

    h0 = jnp.zeros((b, A_GROUPS, r, A_HEAD_DIM, A_STATE), jnp.float32)
    _, prev = lax.scan(step, h0, (states.astype(jnp.float32).swapaxes(0, 1), chunk_decay.swapaxes(0, 1)))
    prev = prev.swapaxes(0, 1)
    y_off = jnp.einsum('bclgn,bcgrpn,bclgr->bclgrp', cm, prev, jnp.exp(a_cs))
    y = y_diag + y_off + xh * d_skip.reshape(A_GROUPS, r, 1)
    y = y.reshape(b, L, A_INNER).astype(xs.dtype)
    return rms_norm(y * jax.nn.silu(z), norm_w)


def compress(kv, pe, w1, w2):
    b, L, hk, d = kv.shape
    n_cmp = (L - CMP_LEN) // CMP_STRIDE + 1
    idx = np.arange(n_cmp)[:, None] * CMP_STRIDE + np.arange(CMP_LEN)[None, :]
    blocks = kv[:, idx] + pe[:, None, :]
    blocks = blocks.transpose(0, 1, 3, 2, 4).reshape(b, n_cmp, hk, CMP_LEN * d)
    return jax.nn.silu(blocks @ w1) @ w2


def nsa_mixer(q, kc, vc, ks, vs, kw, vw, gates, z, cmp_pe, cmp_w1, cmp_w2):
    b, L, _ = q.shape
    r = B_HEADS // B_KV_HEADS
    scale = B_HEAD_DIM ** -0.5
    q = q.reshape(b, L, B_KV_HEADS, r, B_HEAD_DIM)
    kvs = (b, L, B_KV_HEADS, B_HEAD_DIM)
    kc, vc, ks, vs, kw, vw = [t.reshape(kvs) for t in (kc, vc, ks, vs, kw, vw)]
    kc = compress(kc, cmp_pe[0], cmp_w1[0], cmp_w2[0])
    vc = compress(vc, cmp_pe[1], cmp_w1[1], cmp_w2[1])
    n_cmp, n_sel = kc.shape[1], L // SEL_LEN
    top_k = min(SEL_TOPK, n_sel)
    cmp_end = np.arange(n_cmp) * CMP_STRIDE + CMP_LEN - 1
    ci, sj = np.arange(n_cmp)[:, None], np.arange(n_sel)[None, :]
    overlap = jnp.asarray(((ci * CMP_STRIDE < (sj + 1) * SEL_LEN) &
                           (ci * CMP_STRIDE + CMP_LEN > sj * SEL_LEN)).astype(np.float32))
    pad = jnp.zeros((b, WINDOW, B_KV_HEADS, B_HEAD_DIM), kw.dtype)
    kw_p = jnp.concatenate([pad, kw], axis=1)
    vw_p = jnp.concatenate([pad, vw], axis=1)
    gates = jax.nn.sigmoid(gates.astype(jnp.float32)).reshape(b, L, B_KV_HEADS, r, 3)
    bidx = jnp.arange(b)[:, None, None, None]
    hidx = jnp.arange(B_KV_HEADS)[None, None, :, None]
    blk = jnp.arange(n_sel)

    def block(i):
        s0 = i * Q_BLOCK
        qb = lax.dynamic_slice_in_dim(q, s0, Q_BLOCK, axis=1)
        t = s0 + jnp.arange(Q_BLOCK)
        s_c = jnp.einsum('btgrd,bngd->bgrtn', qb, kc) * scale
        p_c = masked_softmax(s_c, cmp_end[None, :] <= t[:, None])
        o_c = jnp.einsum('bgrtn,bngd->btgrd', p_c.astype(vc.dtype), vc)
        imp = jnp.einsum('bgrtn,nj->btgj', p_c, overlap)
        cur = t // SEL_LEN
        eligible = blk[None, :] * SEL_LEN <= t[:, None]
        dist = cur[:, None] - blk[None, :]
        forced = (blk[None, :] == 0) | ((dist >= 0) & (dist < SEL_LOCAL))
        imp = jnp.where(forced[None, :, None, :], jnp.inf,
                        jnp.where(eligible[None, :, None, :], imp, -jnp.inf))
        top_val, top_idx = lax.top_k(imp, top_k)
        tok = (top_idx[..., None] * SEL_LEN + jnp.arange(SEL_LEN)).reshape(b, Q_BLOCK, B_KV_HEADS, top_k * SEL_LEN)
        tok_ok = jnp.repeat(top_val > -jnp.inf, SEL_LEN, axis=-1) & (tok <= t[None, :, None, None])
        k_sel = ks[bidx, tok, hidx]
        v_sel = vs[bidx, tok, hidx]
        s_s = jnp.einsum('btgrd,btgsd->btgrs', qb, k_sel) * scale
        p_s = masked_softmax(s_s, tok_ok[:, :, :, None, :])
        o_s = jnp.einsum('btgrs,btgsd->btgrd', p_s.astype(v_sel.dtype), v_sel)
        kwb = lax.dynamic_slice_in_dim(kw_p, s0, WINDOW + Q_BLOCK, axis=1)
        vwb = lax.dynamic_slice_in_dim(vw_p, s0, WINDOW + Q_BLOCK, axis=1)
        kpos = s0 - WINDOW + jnp.arange(WINDOW + Q_BLOCK)
        m_w = (kpos[None, :] <= t[:, None]) & (kpos[None, :] > t[:, None] - WINDOW) & (kpos[None, :] >= 0)
        s_w = jnp.einsum('btgrd,bsgd->bgrts', qb, kwb) * scale
        p_w = masked_softmax(s_w, m_w)
        o_w = jnp.einsum('bgrts,bsgd->btgrd', p_w.astype(vwb.dtype), vwb)
        gb = lax.dynamic_slice_in_dim(gates, s0, Q_BLOCK, axis=1)
        o = gb[..., 0:1] * o_c + gb[..., 1:2] * o_s + gb[..., 2:3] * o_w
        return o.astype(q.dtype)

    out = lax.map(block, jnp.arange(L // Q_BLOCK))
    out = out.transpose(1, 0, 2, 3, 4, 5).reshape(b, L, BRANCH_WIDTH)
    return out * jax.nn.silu(z)


def rotary(x, pos):
    d = x.shape[-1]
    inv = ROPE_BASE ** (-jnp.arange(0, d, 2, dtype=jnp.float32) / d)
    ang = pos.astype(jnp.float32)[:, None] * inv[None, :]
    cos = jnp.cos(ang)[None, :, None, :]
    sin = jnp.sin(ang)[None, :, None, :]
    xf = x.astype(jnp.float32)
    x1, x2 = xf[..., 0::2], xf[..., 1::2]
    out = jnp.stack([x1 * cos - x2 * sin, x1 * sin + x2 * cos], axis=-1).reshape(x.shape)
    return out.astype(x.dtype)


def retention_mixer(q, k, v, z, gn_w):
    b, L, _ = q.shape
    H, d, T = C_HEADS, C_HEAD_DIM, C_CHUNK
    nc = L // T
    pos = jnp.arange(L)
    q = rotary(q.reshape(b, L, H, d), pos).reshape(b, nc, T, H, d)
    k = (rotary(k.reshape(b, L, H, d), pos) * (d ** -0.5)).reshape(b, nc, T, H, d)
    v = v.reshape(b, nc, T, H, d)
    log_g = jnp.log1p(-jnp.exp2(-5.0 - jnp.arange(H, dtype=jnp.float32)))
    i = jnp.arange(T, dtype=jnp.float32)
    diff = i[:, None] - i[None, :]
    dmat = jnp.where(diff >= 0, jnp.exp(log_g[:, None, None] * jnp.maximum(diff, 0.0)), 0.0)
    s = jnp.einsum('bcihd,bcjhd->bchij', q, k) * dmat
    inner = jnp.einsum('bchij,bcjhe->bcihe', s.astype(v.dtype), v)
    zeta = jnp.exp(log_g[:, None] * (T - 1 - i)[None, :])
    kv = jnp.einsum('bcjhd,hj,bcjhe->bchde', k, zeta, v).astype(jnp.float32)
    chunk_decay = jnp.exp(log_g * T)

    def step(R, kv_c):
        return R * chunk_decay[:, None, None] + kv_c, R

    _, r_prev = lax.scan(step, jnp.zeros((b, H, d, d), jnp.float32), kv.swapaxes(0, 1))
    r_prev = r_prev.swapaxes(0, 1)
    xi = jnp.exp(log_g[:, None] * (i + 1.0)[None, :])
    cross = jnp.einsum('bcihd,hi,bchde->bcihe', q, xi, r_prev)
    o = (inner + cross).reshape(b, L, H, d).astype(jnp.float32)
    mu = jnp.mean(o, axis=-1, keepdims=True)
    var = jnp.mean(jnp.square(o - mu), axis=-1, keepdims=True)
    o = ((o - mu) * lax.rsqrt(var + EPS)).reshape(b, L, H * d) * gn_w.astype(jnp.float32)
    return (jax.nn.silu(z.astype(jnp.float32)) * o).astype(z.dtype)


def setup_inputs(seed: int = 0) -> dict:
    key = jax.random.key(seed)
    kk = jax.random.split(key, 21)
    f32 = jnp.float32

    def nrm(k, shape, fan_in):
        return jax.random.normal(k, shape, f32) * (fan_in ** -0.5)

    def gain(k, shape):
        return 1.0 + 0.02 * jax.random.normal(k, shape, f32)

    dt0 = jnp.exp(jax.random.uniform(kk[7], (DEPTH, A_HEADS), f32, np.log(1e-3), np.log(1e-1)))
    return {
        'x': jax.random.normal(kk[0], (BATCH, SEQ, D_MODEL), f32),
        'p': jax.random.normal(kk[1], (DEPTH, BATCH, SEQ, P_DIM), f32),
        'norm_pre': gain(kk[2], (DEPTH, D_MODEL)),
        'norm_post': gain(kk[3], (DEPTH, D_MODEL)),
        'w_in': nrm(kk[4], (DEPTH, D_MODEL, IN_DIM), D_MODEL),
        'conv_w': nrm(kk[5], (DEPTH, A_CONV, A_CONV_CH), A_CONV),
        'conv_b': 0.01 * jax.random.normal(kk[6], (DEPTH, A_CONV_CH), f32),
        'dt_bias': dt0 + jnp.log(-jnp.expm1(-dt0)),
        'a_log': jnp.log(jax.random.uniform(kk[8], (DEPTH, A_HEADS), f32, 1.0, 16.0)),
        'd_skip': gain(kk[9], (DEPTH, A_HEADS)),
        'ssm_norm': gain(kk[10], (DEPTH, A_INNER)),
        'cmp_pe': 0.02 * jax.random.normal(kk[11], (DEPTH, 2, CMP_LEN, B_HEAD_DIM), f32),
        'cmp_w1': nrm(kk[12], (DEPTH, 2, CMP_LEN * B_HEAD_DIM, CMP_HIDDEN), CMP_LEN * B_HEAD_DIM),
        'cmp_w2': nrm(kk[13], (DEPTH, 2, CMP_HIDDEN, B_HEAD_DIM), CMP_HIDDEN),
        'ret_norm': gain(kk[14], (DEPTH, BRANCH_WIDTH)),
        'w_branch': nrm(kk[15], (DEPTH, N_BRANCH, BRANCH_WIDTH, D_MODEL), BRANCH_WIDTH),
        'w_out': nrm(kk[16], (DEPTH, D_MODEL, D_MODEL), D_MODEL),
        'ple_proj': nrm(kk[17], (DEPTH, P_DIM, D_MODEL), P_DIM),
        'ple_gate': nrm(kk[18], (DEPTH, D_MODEL, D_MODEL), D_MODEL),
        'ple_norm': gain(kk[19], (DEPTH, D_MODEL)),
    }


def reference(x, p, norm_pre, norm_post, w_in, conv_w, conv_b, dt_bias, a_log, d_skip,
              ssm_norm, cmp_pe, cmp_w1, cmp_w2, ret_norm, w_branch, w_out,
              ple_proj, ple_gate, ple_norm):
    b, L, _ = x.shape
    offsets = np.cumsum(IN_SIZES)[:-1].tolist()
    for i in range(DEPTH):
        h = rms_norm(x, norm_pre[i])
        (xa, za, ba, ca, dta, qb, kcb, vcb, ksb, vsb, kwb, vwb, gb, zb,
         qc, kc, vc, zc, gm) = jnp.split(h @ w_in[i], offsets, axis=-1)
        ya = ssd_mixer(xa, za, ba, ca, dta, conv_w[i], conv_b[i], dt_bias[i], a_log[i], d_skip[i], ssm_norm[i])
        yb = nsa_mixer(qb, kcb, vcb, ksb, vsb, kwb, vwb, gb, zb, cmp_pe[i], cmp_w1[i], cmp_w2[i])
        yc = retention_mixer(qc, kc, vc, zc, ret_norm[i])
        u = jnp.einsum('bsjw,jwd->bsjd', jnp.stack([ya, yb, yc], axis=2), w_branch[i])
        gates = jax.nn.sigmoid(gm.astype(jnp.float32)).reshape(b, L, N_BRANCH, D_MODEL)
        merged = jnp.sum(gates * u, axis=2).astype(x.dtype)
        x = x + rms_norm(merged @ w_out[i], norm_post[i])
        e = p[i] @ ple_proj[i]
        g = jax.nn.sigmoid((rms_norm(x) @ ple_gate[i]).astype(jnp.float32)).astype(x.dtype)
        x = x + rms_norm(g * e, ple_norm[i])
    return x
```

```python
import functools

import jax
import jax.numpy as jnp
import numpy as np
from jax import lax
from jax.experimental import pallas as pl
from jax.experimental.pallas import tpu as pltpu

f32 = jnp.float32
bf16 = jnp.bfloat16

D_MODEL = 2048
P_DIM = 256
EPS = 1e-6
NEG = -1e30
WIDTH = 2048

A_HEADS, A_HEAD_DIM, A_GROUPS, A_STATE, A_CONV, CHUNK = 32, 64, 8, 128, 4, 128
A_GW = (A_HEADS // A_GROUPS) * A_HEAD_DIM
A_CONV_CH = WIDTH + 2 * A_GROUPS * A_STATE
B_HEADS, B_KV, B_HD = 16, 4, 128
B_REP = B_HEADS // B_KV
CMP_LEN, CMP_STRIDE, CMP_HIDDEN = 32, 16, 256
SEL_LEN, SEL_TOPK, SEL_LOCAL, WINDOW, QB = 64, 16, 2, 512, 128
SEL_SHIFT = SEL_LEN.bit_length() - 1
KT = 512
C_HEADS, C_HD = 8, 256
ROPE_BASE = 10000.0

IN_SIZES = (2048, 2048, 1024, 1024, 32,
            2048, 512, 512, 512, 512, 512, 512, 48, 2048,
            2048, 2048, 2048, 2048, 6144)

C_XBC, C_ZA, C_QC, C_KC, C_ZC, C_ZB, C_GM = 0, 4096, 6144, 8192, 10240, 12288, 14336
C_KCB, C_VCB, C_GATES, C_DT, N32 = 20480, 20992, 21504, 22016, 22528
H_QB, H_VC, H_KS, H_VS, H_KW, H_VW, N16 = 0, 2048, 4096, 4608, 5120, 5632, 6144

LANES = 128
VMEM_LIMIT = 56 * 1024 * 1024


def _cparams(*sem):
    return pltpu.CompilerParams(dimension_semantics=sem, vmem_limit_bytes=VMEM_LIMIT)


def _sigmoid(x):
    return 1.0 / (1.0 + jnp.exp(-x))


def _silu(x):
    return x * _sigmoid(x)


def _dot(a, b):
    return jnp.dot(a, b, preferred_element_type=f32)


def _dot_nt(a, b):
    return lax.dot_general(a, b, (((1,), (1,)), ((), ())), preferred_element_type=f32)


def _dot_tn(a, b):
    return lax.dot_general(a, b, (((0,), (0,)), ((), ())), preferred_element_type=f32)


def _norm_matmul_body(x_ref, g_ref, w_ref, o_ref, h_ref):
    @pl.when(pl.program_id(1) == 0)
    def _():
        xf = x_ref[...]
        ms = jnp.mean(xf * xf, axis=-1, keepdims=True)
        h_ref[...] = (xf * lax.rsqrt(ms + EPS) * g_ref[...]).astype(bf16)

    o_ref[...] = _dot(h_ref[...], w_ref[...]).astype(o_ref.dtype)


def norm_matmul(x, gain, w, out_dtype, name):
    m, k = x.shape
    n = w.shape[1]
    tm, tn = min(1024, m), 1024
    return pl.pallas_call(
        _norm_matmul_body,
        grid=(m // tm, n // tn),
        in_specs=[pl.BlockSpec((tm, k), lambda i, j: (i, 0)),
                  pl.BlockSpec((1, k), lambda i, j: (0, 0)),
                  pl.BlockSpec((k, tn), lambda i, j: (0, j))],
        out_specs=pl.BlockSpec((tm, tn), lambda i, j: (i, j)),
        out_shape=jax.ShapeDtypeStruct((m, n), out_dtype),
        scratch_shapes=[pltpu.VMEM((tm, k), bf16)],
        compiler_params=_cparams("parallel", "arbitrary"),
        name=name,
    )(x, gain.reshape(1, k), w)


def _expand_heads(v, g, width):
    rows = v.shape[0]
    hd = width // 4
    lane = lax.broadcasted_iota(jnp.int32, (rows, width), 1)
    out = jnp.broadcast_to(v[:, 4 * g + 3:4 * g + 4], (rows, width))
    for r in (2, 1, 0):
        out = jnp.where(lane < (r + 1) * hd, jnp.broadcast_to(v[:, 4 * g + r:4 * g + r + 1], (rows, width)), out)
    return out


def _ssd_body(xbc_ref, z_ref, dt_ref, cw_ref, cb_ref, dtb_ref, alog_ref, dskip_ref, nw_ref,
              o_ref, tail_ref, st_ref, y_ref):
    c = pl.program_id(0)

    @pl.when(c == 0)
    def _():
        tail_ref[...] = jnp.zeros_like(tail_ref)
        st_ref[...] = jnp.zeros_like(st_ref)

    cur = xbc_ref[...]
    tail = tail_ref[...]
    row8 = lax.broadcasted_iota(jnp.int32, (8, A_CONV_CH), 0)
    acc = cur * cw_ref[A_CONV - 1:A_CONV, :] + cb_ref[...]
    for s in range(1, A_CONV):
        rolled = pltpu.roll(cur, s, 0)
        first = jnp.where(row8 < s, pltpu.roll(tail, s, 0), rolled[0:8])
        shifted = jnp.concatenate([first, rolled[8:]], axis=0)
        acc = acc + shifted * cw_ref[A_CONV - 1 - s:A_CONV - s, :]
    tail_ref[...] = cur[CHUNK - 8:CHUNK]
    xbc = _silu(acc)
    xs = xbc[:, 0:WIDTH]
    bm = xbc[:, WIDTH:WIDTH + A_GROUPS * A_STATE]
    cm = xbc[:, WIDTH + A_GROUPS * A_STATE:]

    xdt = dt_ref[...] + dtb_ref[...]
    dt = jnp.maximum(xdt, 0.0) + jnp.log1p(jnp.exp(-jnp.abs(xdt)))
    a = -jnp.exp(alog_ref[...]) * dt
    rowi = lax.broadcasted_iota(jnp.int32, (CHUNK, LANES), 0)
    a_cs = a
    s = 1
    while s < CHUNK:
        a_cs = a_cs + jnp.where(rowi >= s, pltpu.roll(a_cs, s, 0), 0.0)
        s *= 2
    a_cs_t = a_cs.T
    a_last = a_cs[CHUNK - 1:CHUNK, :]
    exp_acs = jnp.exp(a_cs)
    dend = jnp.exp(a_last - a_cs)
    cdecay = jnp.exp(a_last)

    li = lax.broadcasted_iota(jnp.int32, (CHUNK, CHUNK), 0)
    si = lax.broadcasted_iota(jnp.int32, (CHUNK, CHUNK), 1)
    causal = li >= si
    lane_g = lax.broadcasted_iota(jnp.int32, (CHUNK, A_GW), 1)

    for g in range(A_GROUPS):
        cm_g = cm[:, g * A_STATE:(g + 1) * A_STATE].astype(bf16)
        bm_g = bm[:, g * A_STATE:(g + 1) * A_STATE].astype(bf16)
        xs_g = xs[:, g * A_GW:(g + 1) * A_GW]
        cb = _dot_nt(cm_g, bm_g)
        xdt_g = xs_g * _expand_heads(dt, g, A_GW)
        xdt_b = xdt_g.astype(bf16)
        lhs, rhs = [], []
        for r in range(4):
            h = 4 * g + r
            seg = a_cs[:, h:h + 1] - a_cs_t[h:h + 1, :]
            dec = jnp.exp(jnp.where(causal, seg, -jnp.inf))
            lhs.append((cb * dec).astype(bf16))
            keep = (lane_g >= r * A_HEAD_DIM) & (lane_g < (r + 1) * A_HEAD_DIM)
            rhs.append(jnp.where(keep, xdt_b, jnp.zeros_like(xdt_b)))
        y_diag = _dot(jnp.concatenate(lhs, axis=1), jnp.concatenate(rhs, axis=0))
        st = st_ref[g]
        y_off = _dot(cm_g, st.astype(bf16)) * _expand_heads(exp_acs, g, A_GW)
        y_ref[:, g * A_GW:(g + 1) * A_GW] = y_diag + y_off + xs_g * dskip_ref[:, g * A_GW:(g + 1) * A_GW]
        wgt = (xdt_g * _expand_heads(dend, g, A_GW)).astype(bf16)
        st_ref[g] = st * _expand_heads(cdecay, g, A_GW) + _dot_tn(bm_g, wgt)

    yz = y_ref[...] * _silu(z_ref[...])
    ms = jnp.mean(yz * yz, axis=-1, keepdims=True)
    o_ref[...] = (yz * lax.rsqrt(ms + EPS) * nw_ref[...]).astype(o_ref.dtype)


def ssd_mixer(p32, conv_w, conv_b, dt_bias, a_log, d_skip, norm_w):
    L = p32.shape[0]
    pad = LANES - A_HEADS
    dtb = jnp.pad(dt_bias, (0, pad)).reshape(1, LANES)
    alog = jnp.pad(a_log, (0, pad)).reshape(1, LANES)
    dskip = jnp.repeat(d_skip, A_HEAD_DIM).reshape(1, WIDTH)
    const = lambda shape: pl.BlockSpec(shape, lambda c: (0,) * len(shape))
    return pl.pallas_call(
        _ssd_body,
        grid=(L // CHUNK,),
        in_specs=[pl.BlockSpec((CHUNK, A_CONV_CH), lambda c: (c, C_XBC // A_CONV_CH)),
                  pl.BlockSpec((CHUNK, WIDTH), lambda c: (c, C_ZA // WIDTH)),
                  pl.BlockSpec((CHUNK, LANES), lambda c: (c, C_DT // LANES)),
                  const((A_CONV, A_CONV_CH)), const((1, A_CONV_CH)), const((1, LANES)), const((1, LANES)),
                  const((1, WIDTH)), const((1, WIDTH))],
        out_specs=pl.BlockSpec((CHUNK, WIDTH), lambda c: (c, 0)),
        out_shape=jax.ShapeDtypeStruct((L, WIDTH), bf16),
        scratch_shapes=[pltpu.VMEM((8, A_CONV_CH), f32),
                        pltpu.VMEM((A_GROUPS, A_STATE, A_GW), f32),
                        pltpu.VMEM((CHUNK, WIDTH), f32)],
        compiler_params=_cparams("arbitrary"),
        name="ssd_mixer",
    )(p32, p32, p32, conv_w, conv_b.reshape(1, A_CONV_CH), dtb, alog, dskip, norm_w.reshape(1, WIDTH))


def _ret_body(q_ref, k_ref, z_ref, v_ref, cos_ref, sin_ref, dmat_ref, zeta_ref, xi_ref, cd_ref, gn_ref,
              o_ref, r_ref):
    c = pl.program_id(0)

    @pl.when(c == 0)
    def _():
        r_ref[...] = jnp.zeros_like(r_ref)

    cos = cos_ref[...]
    sin = sin_ref[...]
    half = C_HD // 2

    def rot(x):
        x1, x2 = x[:, :half], x[:, half:]
        return jnp.concatenate([x1 * cos - x2 * sin, x1 * sin + x2 * cos], axis=1)

    for h in range(C_HEADS):
        sl = slice(h * C_HD, (h + 1) * C_HD)
        qr = rot(q_ref[:, sl])
        kr = rot(k_ref[:, sl]) * (C_HD ** -0.5)
        qb, kb = qr.astype(bf16), kr.astype(bf16)
        v = v_ref[:, sl]
        s = _dot_nt(qb, kb) * dmat_ref[h]
        inner = _dot(s.astype(bf16), v)
        rst = r_ref[h]
        cross = _dot(qb, rst.astype(bf16)) * xi_ref[:, h:h + 1]
        o = inner + cross
        mu = jnp.mean(o, axis=-1, keepdims=True)
        d = o - mu
        var = jnp.mean(d * d, axis=-1, keepdims=True)
        on = d * lax.rsqrt(var + EPS) * gn_ref[:, sl]
        o_ref[:, sl] = (_silu(z_ref[:, sl]) * on).astype(o_ref.dtype)
        kz = (kr * zeta_ref[:, h:h + 1]).astype(bf16)
        r_ref[h] = rst * cd_ref[0:1, h:h + 1] + _dot_tn(kz, v)


def _retention_tables(L):
    T, H, d = CHUNK, C_HEADS, C_HD
    pos = jnp.arange(L)
    inv = ROPE_BASE ** (-jnp.arange(0, d, 2, dtype=f32) / d)
    ang = pos.astype(f32)[:, None] * inv[None, :]
    log_g = jnp.log1p(-jnp.exp2(-5.0 - jnp.arange(H, dtype=f32)))
    i = jnp.arange(T, dtype=f32)
    diff = i[:, None] - i[None, :]
    dmat = jnp.where(diff >= 0, jnp.exp(log_g[:, None, None] * jnp.maximum(diff, 0.0)), 0.0)
    zeta = jnp.exp(log_g[:, None] * (T - 1 - i)[None, :])
    xi = jnp.exp(log_g[:, None] * (i + 1.0)[None, :])
    cdecay = jnp.exp(log_g * T)
    padl = lambda t: jnp.pad(t, ((0, 0), (0, LANES - H)))
    return (jnp.cos(ang), jnp.sin(ang), dmat, padl(zeta.T), padl(xi.T), padl(cdecay[None, :]))


def retention_mixer(p32, p16, gn_w):
    L = p32.shape[0]
    cos, sin, dmat, zeta_t, xi_t, cdecay = _retention_tables(L)
    const = lambda shape: pl.BlockSpec(shape, lambda c: (0,) * len(shape))
    half = C_HD // 2
    return pl.pallas_call(
        _ret_body,
        grid=(L // CHUNK,),
        in_specs=[pl.BlockSpec((CHUNK, WIDTH), lambda c: (c, C_QC // WIDTH)),
                  pl.BlockSpec((CHUNK, WIDTH), lambda c: (c, C_KC // WIDTH)),
                  pl.BlockSpec((CHUNK, WIDTH), lambda c: (c, C_ZC // WIDTH)),
                  pl.BlockSpec((CHUNK, WIDTH), lambda c: (c, H_VC // WIDTH)),
                  pl.BlockSpec((CHUNK, half), lambda c: (c, 0)),
                  pl.BlockSpec((CHUNK, half), lambda c: (c, 0)),
                  const((C_HEADS, CHUNK, CHUNK)), const((CHUNK, LANES)), const((CHUNK, LANES)),
                  const((1, LANES)), const((1, WIDTH))],
        out_specs=pl.BlockSpec((CHUNK, WIDTH), lambda c: (c, 0)),
        out_shape=jax.ShapeDtypeStruct((L, WIDTH), bf16),
        scratch_shapes=[pltpu.VMEM((C_HEADS, C_HD, C_HD), f32)],
        compiler_params=_cparams("arbitrary"),
        name="retention_mixer",
    )(p32, p32, p32, p16, cos, sin, dmat, zeta_t, xi_t, cdecay, gn_w.reshape(1, WIDTH))


def _compress_body(x_ref, pe_ref, w1_ref, w2_ref, o_ref, acc_ref):
    b = pl.program_id(2)
    nb = pl.num_programs(2)

    @pl.when(b == 0)
    def _():
        acc_ref[...] = jnp.zeros_like(acc_ref)

    x = x_ref[...]
    for a in range(2):
        acc_ref[a] += _dot((x + pe_ref[a]).astype(bf16), w1_ref[a])

    @pl.when(b == nb - 1)
    def _():
        n = acc_ref.shape[1]
        pre = acc_ref[0] + pltpu.roll(acc_ref[1], n - 1, 0)
        out = _dot(_silu(pre).astype(bf16), w2_ref[...])
        row = lax.broadcasted_iota(jnp.int32, out.shape, 0)
        o_ref[...] = jnp.where(row < n - 1, out, 0.0).astype(o_ref.dtype)


def compress_kv(p32, cmp_pe, cmp_w1, cmp_w2):
    L, n32 = p32.shape
    nblk = L // CMP_STRIDE
    xr = p32.reshape(nblk, CMP_STRIDE * n32)
    cpr = n32 // LANES
    w1 = cmp_w1.reshape(2, 2, CMP_STRIDE, B_HD, CMP_HIDDEN).astype(bf16)
    pe = cmp_pe.reshape(2, 2, CMP_STRIDE, 1, B_HD)
    w2 = cmp_w2.astype(bf16)
    return pl.pallas_call(
        _compress_body,
        grid=(2, B_KV, CMP_STRIDE),
        in_specs=[pl.BlockSpec((nblk, LANES), lambda w, h, b: (0, b * cpr + C_KCB // LANES + w * B_KV + h)),
                  pl.BlockSpec((None, 2, None, 1, B_HD), lambda w, h, b: (w, 0, b, 0, 0)),
                  pl.BlockSpec((None, 2, None, B_HD, CMP_HIDDEN), lambda w, h, b: (w, 0, b, 0, 0)),
                  pl.BlockSpec((None, CMP_HIDDEN, B_HD), lambda w, h, b: (w, 0, 0))],
        out_specs=pl.BlockSpec((None, None, nblk, B_HD), lambda w, h, b: (w, h, 0, 0)),
        out_shape=jax.ShapeDtypeStruct((2, B_KV, nblk, B_HD), bf16),
        scratch_shapes=[pltpu.VMEM((2, nblk, CMP_HIDDEN), f32)],
        compiler_params=_cparams("parallel", "parallel", "arbitrary"),
        name="nsa_compress",
    )(xr, pe, w1, w2)


def _masked_softmax(s, mask):
    s = jnp.where(mask, s, NEG)
    e = jnp.exp(s - jnp.max(s, axis=-1, keepdims=True))
    return jnp.where(mask, e / jnp.sum(e, axis=-1, keepdims=True), 0.0)


def _nsa_body(q_ref, kc_ref, vc_ref, ks_ref, vs_ref, kw_ref, vw_ref, gate_ref, z_ref, o_ref, vt_ref):
    i = pl.program_id(1)
    L = ks_ref.shape[0]
    ncmp = kc_ref.shape[0]
    nsel = L // SEL_LEN
    scale = B_HD ** -0.5
    t0 = i * QB
    rows = B_REP * QB

    q = jnp.concatenate([q_ref[:, r * B_HD:(r + 1) * B_HD] for r in range(B_REP)], axis=0)
    t_rows = t0 + (lax.broadcasted_iota(jnp.int32, (rows, 1), 0) & (QB - 1))

    n_idx = lax.broadcasted_iota(jnp.int32, (rows, ncmp), 1)
    p_c = _masked_softmax(_dot_nt(q, kc_ref[...]) * scale, n_idx * CMP_STRIDE + (CMP_LEN - 1) <= t_rows)
    o_c = _dot(p_c.astype(bf16), vc_ref[...])

    psum = p_c[0:QB]
    for r in range(1, B_REP):
        psum = psum + p_c[r * QB:(r + 1) * QB]
    ci = lax.broadcasted_iota(jnp.int32, (ncmp, nsel), 0)
    sj = lax.broadcasted_iota(jnp.int32, (ncmp, nsel), 1)
    overlap = jnp.where((ci * CMP_STRIDE < (sj + 1) * SEL_LEN) & (ci * CMP_STRIDE + CMP_LEN > sj * SEL_LEN),
                        1.0, 0.0).astype(bf16)
    imp = jnp.zeros((QB, nsel), f32)
    rem = psum
    for _ in range(3):
        part = rem.astype(bf16)
        imp = imp + _dot(part, overlap)
        rem = rem - part.astype(f32)
    tq = t0 + lax.broadcasted_iota(jnp.int32, (QB, nsel), 0)
    blk = lax.broadcasted_iota(jnp.int32, (QB, nsel), 1)
    dist = (tq >> SEL_SHIFT) - blk
    forced = (blk == 0) | ((dist >= 0) & (dist < SEL_LOCAL))
    val = jnp.where(forced, jnp.inf, jnp.where(blk * SEL_LEN <= tq, imp, -jnp.inf))
    vt = val.T
    vt_ref[...] = vt
    bsub = lax.broadcasted_iota(jnp.int32, (nsel, QB), 0)

    def rank_step(jp, cnt):
        other = vt_ref[pl.ds(jp, 1), :]
        beats = jnp.where(bsub > jp, jnp.where(other >= vt, 1.0, 0.0), jnp.where(other > vt, 1.0, 0.0))
        return cnt + beats

    cnt = lax.fori_loop(0, nsel, rank_step, jnp.zeros((nsel, QB), f32))
    sel = jnp.where((cnt < float(min(SEL_TOPK, nsel))) & (vt > -jnp.inf), 1.0, 0.0).T.astype(bf16)

    ej = lax.broadcasted_iota(jnp.int32, (nsel, KT), 0)
    eu = lax.broadcasted_iota(jnp.int32, (nsel, KT), 1)
    ku = lax.broadcasted_iota(jnp.int32, (rows, KT), 1)

    def sel_step(kt, carry):
        m, l, acc = carry
        k0 = pl.multiple_of(kt * KT, KT)
        expand = jnp.where(ej == kt * (KT // SEL_LEN) + (eu >> SEL_SHIFT), 1.0, 0.0).astype(bf16)
        picked = _dot(sel, expand)
        picked = jnp.concatenate([picked] * B_REP, axis=0)
        mask = (picked > 0.5) & (k0 + ku <= t_rows)
        s = jnp.where(mask, _dot_nt(q, ks_ref[pl.ds(k0, KT), :]) * scale, NEG)
        m_new = jnp.maximum(m, jnp.max(s, axis=-1, keepdims=True))
        alpha = jnp.exp(m - m_new)
        p = jnp.where(mask, jnp.exp(s - m_new), 0.0)
        l = alpha * l + jnp.sum(p, axis=-1, keepdims=True)
        acc = alpha * acc + _dot(p.astype(bf16), vs_ref[pl.ds(k0, KT), :])
        return m_new, l, acc

    n_tiles = (t0 + QB - 1) // KT + 1
    _, l_s, acc_s = lax.fori_loop(0, n_tiles, sel_step,
                                  (jnp.full((rows, 1), NEG, f32), jnp.zeros((rows, 1), f32),
                                   jnp.zeros((rows, B_HD), f32)))
    o_s = acc_s / l_s

    wlen = WINDOW + QB
    w0 = pl.multiple_of(jnp.maximum(t0 - WINDOW, 0), QB)
    kpos = w0 + lax.broadcasted_iota(jnp.int32, (rows, wlen), 1)
    p_w = _masked_softmax(_dot_nt(q, kw_ref[pl.ds(w0, wlen), :]) * scale,
                          (kpos <= t_rows) & (kpos > t_rows - WINDOW))
    o_w = _dot(p_w.astype(bf16), vw_ref[pl.ds(w0, wlen), :])

    gt = _sigmoid(gate_ref[...])
    for r in range(B_REP):
        rs = slice(r * QB, (r + 1) * QB)
        o = (gt[:, 3 * r:3 * r + 1] * o_c[rs] + gt[:, 3 * r + 1:3 * r + 2] * o_s[rs]
             + gt[:, 3 * r + 2:3 * r + 3] * o_w[rs])
        o_ref[:, r * B_HD:(r + 1) * B_HD] = (o * _silu(z_ref[:, r * B_HD:(r + 1) * B_HD])).astype(o_ref.dtype)


def nsa_mixer(p32, p16, cmp):
    L = p32.shape[0]
    ncmp = cmp.shape[2]
    gw = B_REP * B_HD
    kv = lambda off: pl.BlockSpec((L, B_HD), lambda g, i: (0, off // B_HD + g))
    return pl.pallas_call(
        _nsa_body,
        grid=(B_KV, L // QB),
        in_specs=[pl.BlockSpec((QB, gw), lambda g, i: (i, H_QB // gw + g)),
                  pl.BlockSpec((None, None, ncmp, B_HD), lambda g, i: (0, g, 0, 0)),
                  pl.BlockSpec((None, None, ncmp, B_HD), lambda g, i: (1, g, 0, 0)),
                  kv(H_KS), kv(H_VS), kv(H_KW), kv(H_VW),
                  pl.BlockSpec((QB, LANES), lambda g, i: (i, C_GATES // LANES + g)),
                  pl.BlockSpec((QB, gw), lambda g, i: (i, C_ZB // gw + g))],
        out_specs=pl.BlockSpec((QB, gw), lambda g, i: (i, g)),
        out_shape=jax.ShapeDtypeStruct((L, WIDTH), bf16),
        scratch_shapes=[pltpu.VMEM((L // SEL_LEN, QB), f32)],
        compiler_params=_cparams("parallel", "arbitrary"),
        name="nsa_attention",
    )(p16, cmp, cmp, p16, p16, p16, p16, p32, p32)


def _merge_body(ya_ref, yb_ref, yc_ref, w_ref, ga_ref, gb_ref, gc_ref, o_ref):
    acc = _sigmoid(ga_ref[...]) * _dot(ya_ref[...], w_ref[0])
    acc = acc + _sigmoid(gb_ref[...]) * _dot(yb_ref[...], w_ref[1])
    acc = acc + _sigmoid(gc_ref[...]) * _dot(yc_ref[...], w_ref[2])
    o_ref[...] = acc.astype(o_ref.dtype)


def merge_branches(ya, yb, yc, w_branch, p32):
    L = ya.shape[0]
    tm, tn = min(512, L), 512
    yspec = pl.BlockSpec((tm, WIDTH), lambda j, i: (i, 0))
    gspec = lambda b: pl.BlockSpec((tm, tn), lambda j, i: (i, (C_GM + b * D_MODEL) // tn + j))
    return pl.pallas_call(
        _merge_body,
        grid=(D_MODEL // tn, L // tm),
        in_specs=[yspec, yspec, yspec,
                  pl.BlockSpec((3, WIDTH, tn), lambda j, i: (0, 0, j)),
                  gspec(0), gspec(1), gspec(2)],
        out_specs=pl.BlockSpec((tm, tn), lambda j, i: (i, j)),
        out_shape=jax.ShapeDtypeStruct((L, D_MODEL), bf16),
        compiler_params=_cparams("parallel", "parallel"),
        name="branch_merge",
    )(ya, yb, yc, w_branch.astype(bf16), p32, p32, p32)


def _rms(x):
    return x * lax.rsqrt(jnp.mean(x * x, axis=-1, keepdims=True) + EPS)


def _post_body(x_ref, m_ref, p_ref, wo_ref, wp_ref, wg_ref, npost_ref, nple_ref, o_ref):
    x1 = x_ref[...] + _rms(_dot(m_ref[...], wo_ref[...])) * npost_ref[...]
    e = _dot(p_ref[...].astype(bf16), wp_ref[...])
    g = _sigmoid(_dot(_rms(x1).astype(bf16), wg_ref[...]))
    o_ref[...] = x1 + _rms(g * e) * nple_ref[...]


def post_mixer(x, merged, p, w_out, ple_proj, ple_gate, norm_post, ple_norm):
    L = x.shape[0]
    tm = min(256, L)
    const = lambda shape: pl.BlockSpec(shape, lambda i: (0,) * len(shape))
    return pl.pallas_call(
        _post_body,
        grid=(L // tm,),
        in_specs=[pl.BlockSpec((tm, D_MODEL), lambda i: (i, 0)),
                  pl.BlockSpec((tm, D_MODEL), lambda i: (i, 0)),
                  pl.BlockSpec((tm, P_DIM), lambda i: (i, 0)),
                  const((D_MODEL, D_MODEL)), const((P_DIM, D_MODEL)), const((D_MODEL, D_MODEL)),
                  const((1, D_MODEL)), const((1, D_MODEL))],
        out_specs=pl.BlockSpec((tm, D_MODEL), lambda i: (i, 0)),
        out_shape=jax.ShapeDtypeStruct((L, D_MODEL), f32),
        compiler_params=_cparams("parallel"),
        name="post_mixer",
    )(x, merged, p, w_out.astype(bf16), ple_proj.astype(bf16), ple_gate.astype(bf16),
      norm_post.reshape(1, D_MODEL), ple_norm.reshape(1, D_MODEL))


def pack_w_in(w):
    offs = np.cumsum(IN_SIZES)[:-1].tolist()
    (xa, za, ba, ca, dta, qb, kcb, vcb, ksb, vsb, kwb, vwb, gb, zb, qc, kc, vc, zc, gm) = jnp.split(w, offs, axis=1)
    k = w.shape[0]

    def halves(t):
        return t.reshape(k, C_HEADS, C_HD // 2, 2).transpose(0, 1, 3, 2).reshape(k, WIDTH)

    gates = jnp.pad(gb.reshape(k, B_KV, 3 * B_REP), ((0, 0), (0, 0), (0, LANES - 3 * B_REP))).reshape(k, B_KV * LANES)
    dt = jnp.pad(dta, ((0, 0), (0, LANES - A_HEADS)))
    w32 = jnp.concatenate([xa, ba, ca, za, halves(qc), halves(kc), zc, zb, gm, kcb, vcb, gates, dt,
                           jnp.zeros((k, N32 - C_DT - LANES), w.dtype)], axis=1).astype(bf16)
    w16 = jnp.concatenate([qb, vc, ksb, vsb, kwb, vwb], axis=1).astype(bf16)
    return w32, w16


def layer(x, p, norm_pre, norm_post, w_in, conv_w, conv_b, dt_bias, a_log, d_skip, ssm_norm,
          cmp_pe, cmp_w1, cmp_w2, ret_norm, w_branch, w_out, ple_proj, ple_gate, ple_norm):
    w32, w16 = pack_w_in(w_in)
    p32 = norm_matmul(x, norm_pre, w32, f32, "in_proj_f32")
    p16 = norm_matmul(x, norm_pre, w16, bf16, "in_proj_bf16")
    ya = ssd_mixer(p32, conv_w, conv_b, dt_bias, a_log, d_skip, ssm_norm)
    cmp = compress_kv(p32, cmp_pe, cmp_w1, cmp_w2)
    yb = nsa_mixer(p32, p16, cmp)
    yc = retention_mixer(p32, p16, ret_norm)
    merged = merge_branches(ya, yb, yc, w_branch, p32)
    return post_mixer(x, merged, p, w_out, ple_proj, ple_gate, norm_post, ple_norm)


def kernel(x, p, norm_pre, norm_post, w_in, conv_w, conv_b, dt_bias, a_log, d_skip, ssm_norm, cmp_pe, cmp_w1,
           cmp_w2, ret_norm, w_branch, w_out, ple_proj, ple_gate, ple_norm):
    b, L, d = x.shape
    assert b == 1 and d == D_MODEL
    xf = x.reshape(L, d)
    for i in range(p.shape[0]):
        xf = layer(xf, p[i, 0], norm_pre[i], norm_post[i], w_in[i], conv_w[i], conv_b[i], dt_bias[i], a_log[i],
                   d_skip[i], ssm_norm[i], cmp_pe[i], cmp_w1[i], cmp_w2[i], ret_norm[i], w_branch[i], w_out[i],
                   ple_proj[i], ple_gate[i], ple_norm[i])
    return xf.reshape(b, L, d)
```

```python
import functools

import jax
import jax.numpy as jnp
import numpy as np
from jax import lax
from jax.experimental import pallas as pl
from jax.experimental.pallas import tpu as pltpu

f32 = jnp.float32
bf16 = jnp.bfloat16

D_MODEL = 2048
P_DIM = 256
EPS = 1e-6
NEG = -1e30
WIDTH = 2048

A_HEADS, A_HEAD_DIM, A_GROUPS, A_STATE, A_CONV, CHUNK = 32, 64, 8, 128, 4, 128
A_GW = (A_HEADS // A_GROUPS) * A_HEAD_DIM
A_CONV_CH = WIDTH + 2 * A_GROUPS * A_STATE
B_HEADS, B_KV, B_HD = 16, 4, 128
B_REP = B_HEADS // B_KV
CMP_LEN, CMP_STRIDE, CMP_HIDDEN = 32, 16, 256
SEL_LEN, SEL_TOPK, SEL_LOCAL, WINDOW, QB = 64, 16, 2, 512, 128
SEL_SHIFT = SEL_LEN.bit_length() - 1
assert SEL_LOCAL * SEL_LEN >= QB and QB % SEL_LEN == 0
LOG2E = 1.4426950408889634
KT = 512
C_HEADS, C_HD = 8, 256
ROPE_BASE = 10000.0

IN_SIZES = (2048, 2048, 1024, 1024, 32,
            2048, 512, 512, 512, 512, 512, 512, 48, 2048,
            2048, 2048, 2048, 2048, 6144)

C_XBC, C_ZA, C_QC, C_KC, C_ZC, C_ZB, C_GM = 0, 4096, 6144, 8192, 10240, 12288, 14336
C_KCB, C_VCB, C_GATES, C_DT, N32 = 20480, 20992, 21504, 22016, 22528
H_QB, H_VC, H_KS, H_VS, H_KW, H_VW, N16 = 0, 2048, 4096, 4608, 5120, 5632, 6144

LANES = 128
VMEM_LIMIT = 56 * 1024 * 1024


def _cparams(*sem):
    return pltpu.CompilerParams(dimension_semantics=sem, vmem_limit_bytes=VMEM_LIMIT)


def _sigmoid(x):
    return 1.0 / (1.0 + jnp.exp(-x))


def _silu(x):
    return x * _sigmoid(x)


def _dot(a, b):
    return jnp.dot(a, b, preferred_element_type=f32)


def _dot_nt(a, b):
    return lax.dot_general(a, b, (((1,), (1,)), ((), ())), preferred_element_type=f32)


def _dot_tn(a, b):
    return lax.dot_general(a, b, (((0,), (0,)), ((), ())), preferred_element_type=f32)


def _norm_matmul_body(x_ref, g_ref, w_ref, o_ref, h_ref):
    @pl.when(pl.program_id(1) == 0)
    def _():
        xf = x_ref[...]
        ms = jnp.mean(xf * xf, axis=-1, keepdims=True)
        h_ref[...] = (xf * lax.rsqrt(ms + EPS) * g_ref[...]).astype(bf16)

    o_ref[...] = _dot(h_ref[...], w_ref[...]).astype(o_ref.dtype)


def norm_matmul(x, gain, w, out_dtype, name):
    m, k = x.shape
    n = w.shape[1]
    tm, tn = min(1024, m), 1024
    return pl.pallas_call(
        _norm_matmul_body,
        grid=(m // tm, n // tn),
        in_specs=[pl.BlockSpec((tm, k), lambda i, j: (i, 0)),
                  pl.BlockSpec((1, k), lambda i, j: (0, 0)),
                  pl.BlockSpec((k, tn), lambda i, j: (0, j))],
        out_specs=pl.BlockSpec((tm, tn), lambda i, j: (i, j)),
        out_shape=jax.ShapeDtypeStruct((m, n), out_dtype),
        scratch_shapes=[pltpu.VMEM((tm, k), bf16)],
        compiler_params=_cparams("parallel", "arbitrary"),
        name=name,
    )(x, gain.reshape(1, k), w)


def _expand_heads(v, g, width):
    rows = v.shape[0]
    hd = width // 4
    lane = lax.broadcasted_iota(jnp.int32, (rows, width), 1)
    out = jnp.broadcast_to(v[:, 4 * g + 3:4 * g + 4], (rows, width))
    for r in (2, 1, 0):
        out = jnp.where(lane < (r + 1) * hd, jnp.broadcast_to(v[:, 4 * g + r:4 * g + r + 1], (rows, width)), out)
    return out


def _ssd_body(xbc_ref, z_ref, dt_ref, cw_ref, cb_ref, dtb_ref, alog_ref, dskip_ref, nw_ref,
              o_ref, tail_ref, st_ref, y_ref):
    c = pl.program_id(0)

    @pl.when(c == 0)
    def _():
        tail_ref[...] = jnp.zeros_like(tail_ref)
        st_ref[...] = jnp.zeros_like(st_ref)

    cur = xbc_ref[...]
    tail = tail_ref[...]
    row8 = lax.broadcasted_iota(jnp.int32, (8, A_CONV_CH), 0)
    acc = cur * cw_ref[A_CONV - 1:A_CONV, :] + cb_ref[...]
    for s in range(1, A_CONV):
        rolled = pltpu.roll(cur, s, 0)
        first = jnp.where(row8 < s, pltpu.roll(tail, s, 0), rolled[0:8])
        shifted = jnp.concatenate([first, rolled[8:]], axis=0)
        acc = acc + shifted * cw_ref[A_CONV - 1 - s:A_CONV - s, :]
    tail_ref[...] = cur[CHUNK - 8:CHUNK]
    xbc = _silu(acc)
    xs = xbc[:, 0:WIDTH]
    bm = xbc[:, WIDTH:WIDTH + A_GROUPS * A_STATE]
    cm = xbc[:, WIDTH + A_GROUPS * A_STATE:]

    xdt = dt_ref[...] + dtb_ref[...]
    dt = jnp.maximum(xdt, 0.0) + jnp.log1p(jnp.exp(-jnp.abs(xdt)))
    a = -jnp.exp(alog_ref[...]) * dt
    rowi = lax.broadcasted_iota(jnp.int32, (CHUNK, LANES), 0)
    a_cs = a
    s = 1
    while s < CHUNK:
        a_cs = a_cs + jnp.where(rowi >= s, pltpu.roll(a_cs, s, 0), 0.0)
        s *= 2
    a_cs_t = a_cs.T
    a_last = a_cs[CHUNK - 1:CHUNK, :]
    exp_acs = jnp.exp(a_cs)
    dend = jnp.exp(a_last - a_cs)
    cdecay = jnp.exp(a_last)

    li = lax.broadcasted_iota(jnp.int32, (CHUNK, CHUNK), 0)
    si = lax.broadcasted_iota(jnp.int32, (CHUNK, CHUNK), 1)
    causal = li >= si
    lane_g = lax.broadcasted_iota(jnp.int32, (CHUNK, A_GW), 1)

    for g in range(A_GROUPS):
        cm_g = cm[:, g * A_STATE:(g + 1) * A_STATE].astype(bf16)
        bm_g = bm[:, g * A_STATE:(g + 1) * A_STATE].astype(bf16)
        xs_g = xs[:, g * A_GW:(g + 1) * A_GW]
        cb = _dot_nt(cm_g, bm_g)
        xdt_g = xs_g * _expand_heads(dt, g, A_GW)
        xdt_b = xdt_g.astype(bf16)
        lhs, rhs = [], []
        for r in range(4):
            h = 4 * g + r
            seg = a_cs[:, h:h + 1] - a_cs_t[h:h + 1, :]
            dec = jnp.exp(jnp.where(causal, seg, -jnp.inf))
            lhs.append((cb * dec).astype(bf16))
            keep = (lane_g >= r * A_HEAD_DIM) & (lane_g < (r + 1) * A_HEAD_DIM)
            rhs.append(jnp.where(keep, xdt_b, jnp.zeros_like(xdt_b)))
        y_diag = _dot(jnp.concatenate(lhs, axis=1), jnp.concatenate(rhs, axis=0))
        st = st_ref[g]
        y_off = _dot(cm_g, st.astype(bf16)) * _expand_heads(exp_acs, g, A_GW)
        y_ref[:, g * A_GW:(g + 1) * A_GW] = y_diag + y_off + xs_g * dskip_ref[:, g * A_GW:(g + 1) * A_GW]
        wgt = (xdt_g * _expand_heads(dend, g, A_GW)).astype(bf16)
        st_ref[g] = st * _expand_heads(cdecay, g, A_GW) + _dot_tn(bm_g, wgt)

    yz = y_ref[...] * _silu(z_ref[...])
    ms = jnp.mean(yz * yz, axis=-1, keepdims=True)
    o_ref[...] = (yz * lax.rsqrt(ms + EPS) * nw_ref[...]).astype(o_ref.dtype)


def ssd_mixer(p32, conv_w, conv_b, dt_bias, a_log, d_skip, norm_w):
    L = p32.shape[0]
    pad = LANES - A_HEADS
    dtb = jnp.pad(dt_bias, (0, pad)).reshape(1, LANES)
    alog = jnp.pad(a_log, (0, pad)).reshape(1, LANES)
    dskip = jnp.repeat(d_skip, A_HEAD_DIM).reshape(1, WIDTH)
    const = lambda shape: pl.BlockSpec(shape, lambda c: (0,) * len(shape))
    return pl.pallas_call(
        _ssd_body,
        grid=(L // CHUNK,),
        in_specs=[pl.BlockSpec((CHUNK, A_CONV_CH), lambda c: (c, C_XBC // A_CONV_CH)),
                  pl.BlockSpec((CHUNK, WIDTH), lambda c: (c, C_ZA // WIDTH)),
                  pl.BlockSpec((CHUNK, LANES), lambda c: (c, C_DT // LANES)),
                  const((A_CONV, A_CONV_CH)), const((1, A_CONV_CH)), const((1, LANES)), const((1, LANES)),
                  const((1, WIDTH)), const((1, WIDTH))],
        out_specs=pl.BlockSpec((CHUNK, WIDTH), lambda c: (c, 0)),
        out_shape=jax.ShapeDtypeStruct((L, WIDTH), bf16),
        scratch_shapes=[pltpu.VMEM((8, A_CONV_CH), f32),
                        pltpu.VMEM((A_GROUPS, A_STATE, A_GW), f32),
                        pltpu.VMEM((CHUNK, WIDTH), f32)],
        compiler_params=_cparams("arbitrary"),
        name="ssd_mixer",
    )(p32, p32, p32, conv_w, conv_b.reshape(1, A_CONV_CH), dtb, alog, dskip, norm_w.reshape(1, WIDTH))


def _ret_body(q_ref, k_ref, z_ref, v_ref, cos_ref, sin_ref, dmat_ref, zeta_ref, xi_ref, cd_ref, gn_ref,
              o_ref, r_ref):
    c = pl.program_id(0)

    @pl.when(c == 0)
    def _():
        r_ref[...] = jnp.zeros_like(r_ref)

    cos = cos_ref[...]
    sin = sin_ref[...]
    half = C_HD // 2

    def rot(x):
        x1, x2 = x[:, :half], x[:, half:]
        return jnp.concatenate([x1 * cos - x2 * sin, x1 * sin + x2 * cos], axis=1)

    for h in range(C_HEADS):
        sl = slice(h * C_HD, (h + 1) * C_HD)
        qr = rot(q_ref[:, sl])
        kr = rot(k_ref[:, sl]) * (C_HD ** -0.5)
        qb, kb = qr.astype(bf16), kr.astype(bf16)
        v = v_ref[:, sl]
        s = _dot_nt(qb, kb) * dmat_ref[h]
        inner = _dot(s.astype(bf16), v)
        rst = r_ref[h]
        cross = _dot(qb, rst.astype(bf16)) * xi_ref[:, h:h + 1]
        o = inner + cross
        mu = jnp.mean(o, axis=-1, keepdims=True)
        d = o - mu
        var = jnp.mean(d * d, axis=-1, keepdims=True)
        on = d * lax.rsqrt(var + EPS) * gn_ref[:, sl]
        o_ref[:, sl] = (_silu(z_ref[:, sl]) * on).astype(o_ref.dtype)
        kz = (kr * zeta_ref[:, h:h + 1]).astype(bf16)
        r_ref[h] = rst * cd_ref[0:1, h:h + 1] + _dot_tn(kz, v)


def _retention_tables(L):
    T, H, d = CHUNK, C_HEADS, C_HD
    pos = jnp.arange(L)
    inv = ROPE_BASE ** (-jnp.arange(0, d, 2, dtype=f32) / d)
    ang = pos.astype(f32)[:, None] * inv[None, :]
    log_g = jnp.log1p(-jnp.exp2(-5.0 - jnp.arange(H, dtype=f32)))
    i = jnp.arange(T, dtype=f32)
    diff = i[:, None] - i[None, :]
    dmat = jnp.where(diff >= 0, jnp.exp(log_g[:, None, None] * jnp.maximum(diff, 0.0)), 0.0)
    zeta = jnp.exp(log_g[:, None] * (T - 1 - i)[None, :])
    xi = jnp.exp(log_g[:, None] * (i + 1.0)[None, :])
    cdecay = jnp.exp(log_g * T)
    padl = lambda t: jnp.pad(t, ((0, 0), (0, LANES - H)))
    return (jnp.cos(ang), jnp.sin(ang), dmat, padl(zeta.T), padl(xi.T), padl(cdecay[None, :]))


def retention_mixer(p32, p16, gn_w):
    L = p32.shape[0]
    cos, sin, dmat, zeta_t, xi_t, cdecay = _retention_tables(L)
    const = lambda shape: pl.BlockSpec(shape, lambda c: (0,) * len(shape))
    half = C_HD // 2
    return pl.pallas_call(
        _ret_body,
        grid=(L // CHUNK,),
        in_specs=[pl.BlockSpec((CHUNK, WIDTH), lambda c: (c, C_QC // WIDTH)),
                  pl.BlockSpec((CHUNK, WIDTH), lambda c: (c, C_KC // WIDTH)),
                  pl.BlockSpec((CHUNK, WIDTH), lambda c: (c, C_ZC // WIDTH)),
                  pl.BlockSpec((CHUNK, WIDTH), lambda c: (c, H_VC // WIDTH)),
                  pl.BlockSpec((CHUNK, half), lambda c: (c, 0)),
                  pl.BlockSpec((CHUNK, half), lambda c: (c, 0)),
                  const((C_HEADS, CHUNK, CHUNK)), const((CHUNK, LANES)), const((CHUNK, LANES)),
                  const((1, LANES)), const((1, WIDTH))],
        out_specs=pl.BlockSpec((CHUNK, WIDTH), lambda c: (c, 0)),
        out_shape=jax.ShapeDtypeStruct((L, WIDTH), bf16),
        scratch_shapes=[pltpu.VMEM((C_HEADS, C_HD, C_HD), f32)],
        compiler_params=_cparams("arbitrary"),
        name="retention_mixer",
    )(p32, p32, p32, p16, cos, sin, dmat, zeta_t, xi_t, cdecay, gn_w.reshape(1, WIDTH))


def _compress_body(x_ref, pe_ref, w1_ref, w2_ref, o_ref):
    n = o_ref.shape[0]
    acc = [jnp.zeros((n, CMP_HIDDEN), f32) for _ in range(2)]
    for b in range(CMP_STRIDE):
        x = x_ref[pl.ds(b, n, stride=CMP_STRIDE), :]
        for a in range(2):
            acc[a] = acc[a] + _dot((x + pe_ref[a, b]).astype(bf16), w1_ref[a, b])
    pre = acc[0] + pltpu.roll(acc[1], n - 1, 0)
    out = _dot(_silu(pre).astype(bf16), w2_ref[...])
    row = lax.broadcasted_iota(jnp.int32, out.shape, 0)
    o_ref[...] = jnp.where(row < n - 1, out, 0.0).astype(o_ref.dtype)


def compress_kv(p32, cmp_pe, cmp_w1, cmp_w2):
    L = p32.shape[0]
    nblk = L // CMP_STRIDE
    w1 = cmp_w1.reshape(2, 2, CMP_STRIDE, B_HD, CMP_HIDDEN).astype(bf16)
    pe = cmp_pe.reshape(2, 2, CMP_STRIDE, 1, B_HD)
    w2 = cmp_w2.astype(bf16)
    return pl.pallas_call(
        _compress_body,
        grid=(2, B_KV),
        in_specs=[pl.BlockSpec((L, LANES), lambda w, h: (0, C_KCB // LANES + w * B_KV + h)),
                  pl.BlockSpec((None, 2, CMP_STRIDE, 1, B_HD), lambda w, h: (w, 0, 0, 0, 0)),
                  pl.BlockSpec((None, 2, CMP_STRIDE, B_HD, CMP_HIDDEN), lambda w, h: (w, 0, 0, 0, 0)),
                  pl.BlockSpec((None, CMP_HIDDEN, B_HD), lambda w, h: (w, 0, 0))],
        out_specs=pl.BlockSpec((None, None, nblk, B_HD), lambda w, h: (w, h, 0, 0)),
        out_shape=jax.ShapeDtypeStruct((2, B_KV, nblk, B_HD), bf16),
        compiler_params=_cparams("parallel", "parallel"),
        name="nsa_compress",
    )(p32, pe, w1, w2)


def _masked_softmax(s, mask):
    s = jnp.where(mask, s, NEG)
    e = jnp.exp(s - jnp.max(s, axis=-1, keepdims=True))
    return jnp.where(mask, e / jnp.sum(e, axis=-1, keepdims=True), 0.0)


def _nsa_body(q_ref, kc_ref, vc_ref, ks_ref, vs_ref, kw_ref, vw_ref, gate_ref, z_ref, o_ref, vt_ref):
    i = pl.program_id(1)
    L = ks_ref.shape[0]
    ncmp = kc_ref.shape[0]
    nsel = L // SEL_LEN
    scale = B_HD ** -0.5
    t0 = i * QB
    rows = B_REP * QB

    q = jnp.concatenate([q_ref[:, r * B_HD:(r + 1) * B_HD] for r in range(B_REP)], axis=0)
    t_rows = t0 + (lax.broadcasted_iota(jnp.int32, (rows, 1), 0) & (QB - 1))

    n_idx = lax.broadcasted_iota(jnp.int32, (rows, ncmp), 1)
    p_c = _masked_softmax(_dot_nt(q, kc_ref[...]) * scale, n_idx * CMP_STRIDE + (CMP_LEN - 1) <= t_rows)
    o_c = _dot(p_c.astype(bf16), vc_ref[...])

    psum = p_c[0:QB]
    for r in range(1, B_REP):
        psum = psum + p_c[r * QB:(r + 1) * QB]
    ci = lax.broadcasted_iota(jnp.int32, (ncmp, nsel), 0)
    sj = lax.broadcasted_iota(jnp.int32, (ncmp, nsel), 1)
    overlap = jnp.where((ci * CMP_STRIDE < (sj + 1) * SEL_LEN) & (ci * CMP_STRIDE + CMP_LEN > sj * SEL_LEN),
                        1.0, 0.0).astype(bf16)
    imp = jnp.zeros((QB, nsel), f32)
    rem = psum
    for _ in range(3):
        part = rem.astype(bf16)
        imp = imp + _dot(part, overlap)
        rem = rem - part.astype(f32)
    tq = t0 + lax.broadcasted_iota(jnp.int32, (QB, nsel), 0)
    blk = lax.broadcasted_iota(jnp.int32, (QB, nsel), 1)
    dist = (tq >> SEL_SHIFT) - blk
    forced = (blk == 0) | ((dist >= 0) & (dist < SEL_LOCAL))
    val = jnp.where(forced, jnp.inf, jnp.where(blk * SEL_LEN <= tq, imp, -jnp.inf))
    vt = val.T
    vt_ref[...] = vt
    bsub = lax.broadcasted_iota(jnp.int32, (nsel, QB), 0)

    def rank_step(jp, cnt):
        other = vt_ref[pl.ds(jp, 1), :]
        beats = jnp.where(bsub > jp, jnp.where(other >= vt, 1.0, 0.0), jnp.where(other > vt, 1.0, 0.0))
        return cnt + beats

    n_elig = jnp.minimum((t0 + QB - 1) // SEL_LEN + 1, nsel)
    cnt = lax.fori_loop(0, n_elig, rank_step, jnp.zeros((nsel, QB), f32))
    picked_t = (cnt < float(min(SEL_TOPK, nsel))) & (vt > -jnp.inf)

    c2 = scale * LOG2E
    td = pl.multiple_of(t0, QB)
    lane_d = lax.broadcasted_iota(jnp.int32, (rows, QB), 1)
    s_d = jnp.where(t0 + lane_d <= t_rows, _dot_nt(q, ks_ref[pl.ds(td, QB), :]) * c2, NEG)
    m_d = jnp.max(s_d, axis=-1, keepdims=True)
    p_d = jnp.exp2(s_d - m_d)
    l_d = jnp.sum(p_d, axis=-1, keepdims=True)
    acc_d = _dot(p_d.astype(bf16), vs_ref[pl.ds(td, QB), :])
    bias_t = jnp.where(picked_t & (bsub * SEL_LEN < t0), 0.0, NEG)
    bias = bias_t.T.astype(bf16)
    qa = jnp.concatenate([q, jnp.concatenate([bias] * B_REP, axis=0)], axis=1)
    eu = lax.broadcasted_iota(jnp.int32, (KT, nsel), 0)
    ej = lax.broadcasted_iota(jnp.int32, (KT, nsel), 1)

    def sel_step(kt, carry):
        m, l, acc = carry
        k0 = pl.multiple_of(kt * KT, KT)
        onehot = jnp.where(ej == kt * (KT // SEL_LEN) + (eu >> SEL_SHIFT), 1.0, 0.0).astype(bf16)
        ka = jnp.concatenate([ks_ref[pl.ds(k0, KT), :], onehot], axis=1)
        s = _dot_nt(qa, ka) * c2
        m_new = jnp.maximum(m, jnp.max(s, axis=-1, keepdims=True))
        alpha = jnp.exp2(m - m_new)
        p = jnp.exp2(s - m_new)
        l = alpha * l + jnp.sum(p, axis=-1, keepdims=True)
        acc = alpha * acc + _dot(p.astype(bf16), vs_ref[pl.ds(k0, KT), :])
        return m_new, l, acc

    _, l_s, acc_s = lax.fori_loop(0, (t0 + KT - 1) // KT, sel_step, (m_d, l_d, acc_d))
    o_s = acc_s / l_s

    wlen = WINDOW + QB
    w0 = pl.multiple_of(jnp.maximum(t0 - WINDOW, 0), QB)
    kpos = w0 + lax.broadcasted_iota(jnp.int32, (rows, wlen), 1)
    p_w = _masked_softmax(_dot_nt(q, kw_ref[pl.ds(w0, wlen), :]) * scale,
                          (kpos <= t_rows) & (kpos > t_rows - WINDOW))
    o_w = _dot(p_w.astype(bf16), vw_ref[pl.ds(w0, wlen), :])

    gt = _sigmoid(gate_ref[...])
    for r in range(B_REP):
        rs = slice(r * QB, (r + 1) * QB)
        o = (gt[:, 3 * r:3 * r + 1] * o_c[rs] + gt[:, 3 * r + 1:3 * r + 2] * o_s[rs]
             + gt[:, 3 * r + 2:3 * r + 3] * o_w[rs])
        o_ref[:, r * B_HD:(r + 1) * B_HD] = (o * _silu(z_ref[:, r * B_HD:(r + 1) * B_HD])).astype(o_ref.dtype)


def nsa_mixer(p32, p16, cmp):
    L = p32.shape[0]
    ncmp = cmp.shape[2]
    gw = B_REP * B_HD
    kv = lambda off: pl.BlockSpec((L, B_HD), lambda g, i: (0, off // B_HD + g))
    return pl.pallas_call(
        _nsa_body,
        grid=(B_KV, L // QB),
        in_specs=[pl.BlockSpec((QB, gw), lambda g, i: (i, H_QB // gw + g)),
                  pl.BlockSpec((None, None, ncmp, B_HD), lambda g, i: (0, g, 0, 0)),
                  pl.BlockSpec((None, None, ncmp, B_HD), lambda g, i: (1, g, 0, 0)),
                  kv(H_KS), kv(H_VS), kv(H_KW), kv(H_VW),
                  pl.BlockSpec((QB, LANES), lambda g, i: (i, C_GATES // LANES + g)),
                  pl.BlockSpec((QB, gw), lambda g, i: (i, C_ZB // gw + g))],
        out_specs=pl.BlockSpec((QB, gw), lambda g, i: (i, g)),
        out_shape=jax.ShapeDtypeStruct((L, WIDTH), bf16),
        scratch_shapes=[pltpu.VMEM((L // SEL_LEN, QB), f32)],
        compiler_params=_cparams("parallel", "arbitrary"),
        name="nsa_attention",
    )(p16, cmp, cmp, p16, p16, p16, p16, p32, p32)


def _merge_body(ya_ref, yb_ref, yc_ref, w_ref, ga_ref, gb_ref, gc_ref, o_ref):
    acc = _sigmoid(ga_ref[...]) * _dot(ya_ref[...], w_ref[0])
    acc = acc + _sigmoid(gb_ref[...]) * _dot(yb_ref[...], w_ref[1])
    acc = acc + _sigmoid(gc_ref[...]) * _dot(yc_ref[...], w_ref[2])
    o_ref[...] = acc.astype(o_ref.dtype)


def merge_branches(ya, yb, yc, w_branch, p32):
    L = ya.shape[0]
    tm, tn = min(512, L), 512
    yspec = pl.BlockSpec((tm, WIDTH), lambda j, i: (i, 0))
    gspec = lambda b: pl.BlockSpec((tm, tn), lambda j, i: (i, (C_GM + b * D_MODEL) // tn + j))
    return pl.pallas_call(
        _merge_body,
        grid=(D_MODEL // tn, L // tm),
        in_specs=[yspec, yspec, yspec,
                  pl.BlockSpec((3, WIDTH, tn), lambda j, i: (0, 0, j)),
                  gspec(0), gspec(1), gspec(2)],
        out_specs=pl.BlockSpec((tm, tn), lambda j, i: (i, j)),
        out_shape=jax.ShapeDtypeStruct((L, D_MODEL), bf16),
        compiler_params=_cparams("parallel", "parallel"),
        name="branch_merge",
    )(ya, yb, yc, w_branch.astype(bf16), p32, p32, p32)


def _rms(x):
    return x * lax.rsqrt(jnp.mean(x * x, axis=-1, keepdims=True) + EPS)


def _post_body(x_ref, m_ref, p_ref, wo_ref, wp_ref, wg_ref, npost_ref, nple_ref, o_ref):
    x1 = x_ref[...] + _rms(_dot(m_ref[...], wo_ref[...])) * npost_ref[...]
    e = _dot(p_ref[...].astype(bf16), wp_ref[...])
    g = _sigmoid(_dot(_rms(x1).astype(bf16), wg_ref[...]))
    o_ref[...] = x1 + _rms(g * e) * nple_ref[...]


def post_mixer(x, merged, p, w_out, ple_proj, ple_gate, norm_post, ple_norm):
    L = x.shape[0]
    tm = min(256, L)
    const = lambda shape: pl.BlockSpec(shape, lambda i: (0,) * len(shape))
    return pl.pallas_call(
        _post_body,
        grid=(L // tm,),
        in_specs=[pl.BlockSpec((tm, D_MODEL), lambda i: (i, 0)),
                  pl.BlockSpec((tm, D_MODEL), lambda i: (i, 0)),
                  pl.BlockSpec((tm, P_DIM), lambda i: (i, 0)),
                  const((D_MODEL, D_MODEL)), const((P_DIM, D_MODEL)), const((D_MODEL, D_MODEL)),
                  const((1, D_MODEL)), const((1, D_MODEL))],
        out_specs=pl.BlockSpec((tm, D_MODEL), lambda i: (i, 0)),
        out_shape=jax.ShapeDtypeStruct((L, D_MODEL), f32),
        compiler_params=_cparams("parallel"),
        name="post_mixer",
    )(x, merged, p, w_out.astype(bf16), ple_proj.astype(bf16), ple_gate.astype(bf16),
      norm_post.reshape(1, D_MODEL), ple_norm.reshape(1, D_MODEL))


def pack_w_in(w):
    offs = np.cumsum(IN_SIZES)[:-1].tolist()
    (xa, za, ba, ca, dta, qb, kcb, vcb, ksb, vsb, kwb, vwb, gb, zb, qc, kc, vc, zc, gm) = jnp.split(w, offs, axis=1)
    k = w.shape[0]

    def halves(t):
        return t.reshape(k, C_HEADS, C_HD // 2, 2).transpose(0, 1, 3, 2).reshape(k, WIDTH)

    gates = jnp.pad(gb.reshape(k, B_KV, 3 * B_REP), ((0, 0), (0, 0), (0, LANES - 3 * B_REP))).reshape(k, B_KV * LANES)
    dt = jnp.pad(dta, ((0, 0), (0, LANES - A_HEADS)))
    w32 = jnp.concatenate([xa, ba, ca, za, halves(qc), halves(kc), zc, zb, gm, kcb, vcb, gates, dt,
                           jnp.zeros((k, N32 - C_DT - LANES), w.dtype)], axis=1).astype(bf16)
    w16 = jnp.concatenate([qb, vc, ksb, vsb, kwb, vwb], axis=1).astype(bf16)
    return w32, w16


def layer(x, p, norm_pre, norm_post, w_in, conv_w, conv_b, dt_bias, a_log, d_skip, ssm_norm,
          cmp_pe, cmp_w1, cmp_w2, ret_norm, w_branch, w_out, ple_proj, ple_gate, ple_norm):
    w32, w16 = pack_w_in(w_in)
    p32 = norm_matmul(x, norm_pre, w32, f32, "in_proj_f32")
    p16 = norm_matmul(x, norm_pre, w16, bf16, "in_proj_bf16")
    ya = ssd_mixer(p32, conv_w, conv_b, dt_bias, a_log, d_skip, ssm_norm)
    cmp = compress_kv(p32, cmp_pe, cmp_w1, cmp_w2)
    yb = nsa_mixer(p32, p16, cmp)
    yc = retention_mixer(p32, p16, ret_norm)
    merged = merge_branches(ya, yb, yc, w_branch, p32)
    return post_mixer(x, merged, p, w_out, ple_proj, ple_gate, norm_post, ple_norm)


def kernel(x, p, norm_pre, norm_post, w_in, conv_w, conv_b, dt_bias, a_log, d_skip, ssm_norm, cmp_pe, cmp_w1,
           cmp_w2, ret_norm, w_branch, w_out, ple_proj, ple_gate, ple_norm):
    b, L, d = x.shape
    assert b == 1 and d == D_MODEL
    xf = x.reshape(L, d)
    for i in range(p.shape[0]):
        xf = layer(xf, p[i, 0], norm_pre[i], norm_post[i], w_in[i], conv_w[i], conv_b[i], dt_bias[i], a_log[i],
                   d_skip[i], ssm_norm[i], cmp_pe[i], cmp_w1[i], cmp_w2[i], ret_norm[i], w_branch[i], w_out[i],
                   ple_proj[i], ple_gate[i], ple_norm[i])
    return xf.reshape(b, L, d)
```

```python
import functools

import jax
import jax.numpy as jnp
import numpy as np
from jax import lax
from jax.experimental import pallas as pl
from jax.experimental.pallas import tpu as pltpu

f32 = jnp.float32
bf16 = jnp.bfloat16

D_MODEL = 2048
P_DIM = 256
EPS = 1e-6
NEG = -1e30
WIDTH = 2048

A_HEADS, A_HEAD_DIM, A_GROUPS, A_STATE, A_CONV, CHUNK = 32, 64, 8, 128, 4, 128
A_GW = (A_HEADS // A_GROUPS) * A_HEAD_DIM
A_CONV_CH = WIDTH + 2 * A_GROUPS * A_STATE
B_HEADS, B_KV, B_HD = 16, 4, 128
B_REP = B_HEADS // B_KV
CMP_LEN, CMP_STRIDE, CMP_HIDDEN = 32, 16, 256
SEL_LEN, SEL_TOPK, SEL_LOCAL, WINDOW, QB = 64, 16, 2, 512, 128
SEL_SHIFT = SEL_LEN.bit_length() - 1
assert SEL_LOCAL * SEL_LEN >= QB and QB % SEL_LEN == 0
LOG2E = 1.4426950408889634
KT = 1024
SEL_UNROLL = 2
C_HEADS, C_HD = 8, 256
ROPE_BASE = 10000.0

IN_SIZES = (2048, 2048, 1024, 1024, 32,
            2048, 512, 512, 512, 512, 512, 512, 48, 2048,
            2048, 2048, 2048, 2048, 6144)

C_XBC, C_ZA, C_QC, C_KC, C_ZC, C_ZB, C_GM = 0, 4096, 6144, 8192, 10240, 12288, 14336
C_KCB, C_VCB, C_GATES, C_DT, N32 = 20480, 20992, 21504, 22016, 22528
H_QB, H_VC, H_KS, H_VS, H_KW, H_VW, N16 = 0, 2048, 4096, 4608, 5120, 5632, 6144

LANES = 128
VMEM_LIMIT = 56 * 1024 * 1024


def _cparams(*sem):
    return pltpu.CompilerParams(dimension_semantics=sem, vmem_limit_bytes=VMEM_LIMIT)


def _sigmoid(x):
    return 1.0 / (1.0 + jnp.exp(-x))


def _silu(x):
    return x * _sigmoid(x)


def _dot(a, b):
    return jnp.dot(a, b, preferred_element_type=f32)


def _dot_nt(a, b):
    return lax.dot_general(a, b, (((1,), (1,)), ((), ())), preferred_element_type=f32)


def _dot_tn(a, b):
    return lax.dot_general(a, b, (((0,), (0,)), ((), ())), preferred_element_type=f32)


def _norm_matmul_body(x_ref, g_ref, w_ref, o_ref, h_ref):
    @pl.when(pl.program_id(1) == 0)
    def _():
        xf = x_ref[...]
        ms = jnp.mean(xf * xf, axis=-1, keepdims=True)
        h_ref[...] = (xf * lax.rsqrt(ms + EPS) * g_ref[...]).astype(bf16)

    o_ref[...] = _dot(h_ref[...], w_ref[...]).astype(o_ref.dtype)


def norm_matmul(x, gain, w, out_dtype, name):
    m, k = x.shape
    n = w.shape[1]
    tm, tn = min(1024, m), 1024
    return pl.pallas_call(
        _norm_matmul_body,
        grid=(m // tm, n // tn),
        in_specs=[pl.BlockSpec((tm, k), lambda i, j: (i, 0)),
                  pl.BlockSpec((1, k), lambda i, j: (0, 0)),
                  pl.BlockSpec((k, tn), lambda i, j: (0, j))],
        out_specs=pl.BlockSpec((tm, tn), lambda i, j: (i, j)),
        out_shape=jax.ShapeDtypeStruct((m, n), out_dtype),
        scratch_shapes=[pltpu.VMEM((tm, k), bf16)],
        compiler_params=_cparams("parallel", "arbitrary"),
        name=name,
    )(x, gain.reshape(1, k), w)


def _expand_heads(v, g, width):
    rows = v.shape[0]
    hd = width // 4
    lane = lax.broadcasted_iota(jnp.int32, (rows, width), 1)
    out = jnp.broadcast_to(v[:, 4 * g + 3:4 * g + 4], (rows, width))
    for r in (2, 1, 0):
        out = jnp.where(lane < (r + 1) * hd, jnp.broadcast_to(v[:, 4 * g + r:4 * g + r + 1], (rows, width)), out)
    return out


def _ssd_body(xbc_ref, z_ref, dt_ref, cw_ref, cb_ref, dtb_ref, alog_ref, dskip_ref, nw_ref,
              o_ref, tail_ref, st_ref, y_ref):
    c = pl.program_id(0)

    @pl.when(c == 0)
    def _():
        tail_ref[...] = jnp.zeros_like(tail_ref)
        st_ref[...] = jnp.zeros_like(st_ref)

    cur = xbc_ref[...]
    tail = tail_ref[...]
    row8 = lax.broadcasted_iota(jnp.int32, (8, A_CONV_CH), 0)
    acc = cur * cw_ref[A_CONV - 1:A_CONV, :] + cb_ref[...]
    for s in range(1, A_CONV):
        rolled = pltpu.roll(cur, s, 0)
        first = jnp.where(row8 < s, pltpu.roll(tail, s, 0), rolled[0:8])
        shifted = jnp.concatenate([first, rolled[8:]], axis=0)
        acc = acc + shifted * cw_ref[A_CONV - 1 - s:A_CONV - s, :]
    tail_ref[...] = cur[CHUNK - 8:CHUNK]
    xbc = _silu(acc)
    xs = xbc[:, 0:WIDTH]
    bm = xbc[:, WIDTH:WIDTH + A_GROUPS * A_STATE]
    cm = xbc[:, WIDTH + A_GROUPS * A_STATE:]

    xdt = dt_ref[...] + dtb_ref[...]
    dt = jnp.maximum(xdt, 0.0) + jnp.log1p(jnp.exp(-jnp.abs(xdt)))
    a = -jnp.exp(alog_ref[...]) * dt
    rowi = lax.broadcasted_iota(jnp.int32, (CHUNK, LANES), 0)
    a_cs = a
    s = 1
    while s < CHUNK:
        a_cs = a_cs + jnp.where(rowi >= s, pltpu.roll(a_cs, s, 0), 0.0)
        s *= 2
    a_cs_t = a_cs.T
    a_last = a_cs[CHUNK - 1:CHUNK, :]
    exp_acs = jnp.exp(a_cs)
    dend = jnp.exp(a_last - a_cs)
    cdecay = jnp.exp(a_last)

    li = lax.broadcasted_iota(jnp.int32, (CHUNK, CHUNK), 0)
    si = lax.broadcasted_iota(jnp.int32, (CHUNK, CHUNK), 1)
    causal = li >= si
    lane_g = lax.broadcasted_iota(jnp.int32, (CHUNK, A_GW), 1)

    for g in range(A_GROUPS):
        cm_g = cm[:, g * A_STATE:(g + 1) * A_STATE].astype(bf16)
        bm_g = bm[:, g * A_STATE:(g + 1) * A_STATE].astype(bf16)
        xs_g = xs[:, g * A_GW:(g + 1) * A_GW]
        cb = _dot_nt(cm_g, bm_g)
        xdt_g = xs_g * _expand_heads(dt, g, A_GW)
        xdt_b = xdt_g.astype(bf16)
        lhs, rhs = [], []
        for r in range(4):
            h = 4 * g + r
            seg = a_cs[:, h:h + 1] - a_cs_t[h:h + 1, :]
            dec = jnp.exp(jnp.where(causal, seg, -jnp.inf))
            lhs.append((cb * dec).astype(bf16))
            keep = (lane_g >= r * A_HEAD_DIM) & (lane_g < (r + 1) * A_HEAD_DIM)
            rhs.append(jnp.where(keep, xdt_b, jnp.zeros_like(xdt_b)))
        y_diag = _dot(jnp.concatenate(lhs, axis=1), jnp.concatenate(rhs, axis=0))
        st = st_ref[g]
        y_off = _dot(cm_g, st.astype(bf16)) * _expand_heads(exp_acs, g, A_GW)
        y_ref[:, g * A_GW:(g + 1) * A_GW] = y_diag + y_off + xs_g * dskip_ref[:, g * A_GW:(g + 1) * A_GW]
        wgt = (xdt_g * _expand_heads(dend, g, A_GW)).astype(bf16)
        st_ref[g] = st * _expand_heads(cdecay, g, A_GW) + _dot_tn(bm_g, wgt)

    yz = y_ref[...] * _silu(z_ref[...])
    ms = jnp.mean(yz * yz, axis=-1, keepdims=True)
    o_ref[...] = (yz * lax.rsqrt(ms + EPS) * nw_ref[...]).astype(o_ref.dtype)


def ssd_mixer(p32, conv_w, conv_b, dt_bias, a_log, d_skip, norm_w):
    L = p32.shape[0]
    pad = LANES - A_HEADS
    dtb = jnp.pad(dt_bias, (0, pad)).reshape(1, LANES)
    alog = jnp.pad(a_log, (0, pad)).reshape(1, LANES)
    dskip = jnp.repeat(d_skip, A_HEAD_DIM).reshape(1, WIDTH)
    const = lambda shape: pl.BlockSpec(shape, lambda c: (0,) * len(shape))
    return pl.pallas_call(
        _ssd_body,
        grid=(L // CHUNK,),
        in_specs=[pl.BlockSpec((CHUNK, A_CONV_CH), lambda c: (c, C_XBC // A_CONV_CH)),
                  pl.BlockSpec((CHUNK, WIDTH), lambda c: (c, C_ZA // WIDTH)),
                  pl.BlockSpec((CHUNK, LANES), lambda c: (c, C_DT // LANES)),
                  const((A_CONV, A_CONV_CH)), const((1, A_CONV_CH)), const((1, LANES)), const((1, LANES)),
                  const((1, WIDTH)), const((1, WIDTH))],
        out_specs=pl.BlockSpec((CHUNK, WIDTH), lambda c: (c, 0)),
        out_shape=jax.ShapeDtypeStruct((L, WIDTH), bf16),
        scratch_shapes=[pltpu.VMEM((8, A_CONV_CH), f32),
                        pltpu.VMEM((A_GROUPS, A_STATE, A_GW), f32),
                        pltpu.VMEM((CHUNK, WIDTH), f32)],
        compiler_params=_cparams("arbitrary"),
        name="ssd_mixer",
    )(p32, p32, p32, conv_w, conv_b.reshape(1, A_CONV_CH), dtb, alog, dskip, norm_w.reshape(1, WIDTH))


def _ret_body(q_ref, k_ref, z_ref, v_ref, cos_ref, sin_ref, dmat_ref, zeta_ref, xi_ref, cd_ref, gn_ref,
              o_ref, r_ref):
    c = pl.program_id(0)

    @pl.when(c == 0)
    def _():
        r_ref[...] = jnp.zeros_like(r_ref)

    cos = cos_ref[...]
    sin = sin_ref[...]
    even = (lax.broadcasted_iota(jnp.int32, (CHUNK, C_HD), 1) & 1) == 0

    def rot(x):
        partner = jnp.where(even, pltpu.roll(x, C_HD - 1, 1), pltpu.roll(x, 1, 1))
        return x * cos + partner * sin

    for h in range(C_HEADS):
        sl = slice(h * C_HD, (h + 1) * C_HD)
        qr = rot(q_ref[:, sl])
        kr = rot(k_ref[:, sl]) * (C_HD ** -0.5)
        qb, kb = qr.astype(bf16), kr.astype(bf16)
        v = v_ref[:, sl]
        s = _dot_nt(qb, kb) * dmat_ref[h]
        inner = _dot(s.astype(bf16), v)
        rst = r_ref[h]
        cross = _dot(qb, rst.astype(bf16)) * xi_ref[:, h:h + 1]
        o = inner + cross
        mu = jnp.mean(o, axis=-1, keepdims=True)
        d = o - mu
        var = jnp.mean(d * d, axis=-1, keepdims=True)
        on = d * lax.rsqrt(var + EPS) * gn_ref[:, sl]
        o_ref[:, sl] = (_silu(z_ref[:, sl]) * on).astype(o_ref.dtype)
        kz = (kr * zeta_ref[:, h:h + 1]).astype(bf16)
        r_ref[h] = rst * cd_ref[0:1, h:h + 1] + _dot_tn(kz, v)


def _retention_tables(L):
    T, H, d = CHUNK, C_HEADS, C_HD
    pos = jnp.arange(L)
    inv = ROPE_BASE ** (-jnp.arange(0, d, 2, dtype=f32) / d)
    ang = pos.astype(f32)[:, None] * inv[None, :]
    log_g = jnp.log1p(-jnp.exp2(-5.0 - jnp.arange(H, dtype=f32)))
    i = jnp.arange(T, dtype=f32)
    diff = i[:, None] - i[None, :]
    dmat = jnp.where(diff >= 0, jnp.exp(log_g[:, None, None] * jnp.maximum(diff, 0.0)), 0.0)
    zeta = jnp.exp(log_g[:, None] * (T - 1 - i)[None, :])
    xi = jnp.exp(log_g[:, None] * (i + 1.0)[None, :])
    cdecay = jnp.exp(log_g * T)
    padl = lambda t: jnp.pad(t, ((0, 0), (0, LANES - H)))
    cos, sin = jnp.cos(ang), jnp.sin(ang)
    cos2 = jnp.stack([cos, cos], axis=-1).reshape(L, d)
    sin2 = jnp.stack([-sin, sin], axis=-1).reshape(L, d)
    return (cos2, sin2, dmat, padl(zeta.T), padl(xi.T), padl(cdecay[None, :]))


def retention_mixer(p32, p16, gn_w):
    L = p32.shape[0]
    cos, sin, dmat, zeta_t, xi_t, cdecay = _retention_tables(L)
    const = lambda shape: pl.BlockSpec(shape, lambda c: (0,) * len(shape))
    return pl.pallas_call(
        _ret_body,
        grid=(L // CHUNK,),
        in_specs=[pl.BlockSpec((CHUNK, WIDTH), lambda c: (c, C_QC // WIDTH)),
                  pl.BlockSpec((CHUNK, WIDTH), lambda c: (c, C_KC // WIDTH)),
                  pl.BlockSpec((CHUNK, WIDTH), lambda c: (c, C_ZC // WIDTH)),
                  pl.BlockSpec((CHUNK, WIDTH), lambda c: (c, H_VC // WIDTH)),
                  pl.BlockSpec((CHUNK, C_HD), lambda c: (c, 0)),
                  pl.BlockSpec((CHUNK, C_HD), lambda c: (c, 0)),
                  const((C_HEADS, CHUNK, CHUNK)), const((CHUNK, LANES)), const((CHUNK, LANES)),
                  const((1, LANES)), const((1, WIDTH))],
        out_specs=pl.BlockSpec((CHUNK, WIDTH), lambda c: (c, 0)),
        out_shape=jax.ShapeDtypeStruct((L, WIDTH), bf16),
        scratch_shapes=[pltpu.VMEM((C_HEADS, C_HD, C_HD), f32)],
        compiler_params=_cparams("arbitrary"),
        name="retention_mixer",
    )(p32, p32, p32, p16, cos, sin, dmat, zeta_t, xi_t, cdecay, gn_w.reshape(1, WIDTH))


def _compress_body(x_ref, pe_ref, w1_ref, w2_ref, o_ref):
    n = o_ref.shape[0]
    acc = [jnp.zeros((n, CMP_HIDDEN), f32) for _ in range(2)]
    for b in range(CMP_STRIDE):
        x = x_ref[pl.ds(b, n, stride=CMP_STRIDE), :]
        for a in range(2):
            acc[a] = acc[a] + _dot((x + pe_ref[a, b]).astype(bf16), w1_ref[a, b])
    pre = acc[0] + pltpu.roll(acc[1], n - 1, 0)
    out = _dot(_silu(pre).astype(bf16), w2_ref[...])
    row = lax.broadcasted_iota(jnp.int32, out.shape, 0)
    o_ref[...] = jnp.where(row < n - 1, out, 0.0).astype(o_ref.dtype)


def compress_kv(p32, cmp_pe, cmp_w1, cmp_w2):
    L = p32.shape[0]
    nblk = L // CMP_STRIDE
    w1 = cmp_w1.reshape(2, 2, CMP_STRIDE, B_HD, CMP_HIDDEN).astype(bf16)
    pe = cmp_pe.reshape(2, 2, CMP_STRIDE, 1, B_HD)
    w2 = cmp_w2.astype(bf16)
    return pl.pallas_call(
        _compress_body,
        grid=(2, B_KV),
        in_specs=[pl.BlockSpec((L, LANES), lambda w, h: (0, C_KCB // LANES + w * B_KV + h)),
                  pl.BlockSpec((None, 2, CMP_STRIDE, 1, B_HD), lambda w, h: (w, 0, 0, 0, 0)),
                  pl.BlockSpec((None, 2, CMP_STRIDE, B_HD, CMP_HIDDEN), lambda w, h: (w, 0, 0, 0, 0)),
                  pl.BlockSpec((None, CMP_HIDDEN, B_HD), lambda w, h: (w, 0, 0))],
        out_specs=pl.BlockSpec((None, None, nblk, B_HD), lambda w, h: (w, h, 0, 0)),
        out_shape=jax.ShapeDtypeStruct((2, B_KV, nblk, B_HD), bf16),
        compiler_params=_cparams("parallel", "parallel"),
        name="nsa_compress",
    )(p32, pe, w1, w2)


def _sort16_network():
    n, pairs, p = 16, [], 1
    while p < n:
        k = p
        while k >= 1:
            for j in range(k % p, n - k, 2 * k):
                for i in range(min(k, n - j - k)):
                    if (i + j) // (2 * p) == (i + j + k) // (2 * p):
                        pairs.append((i + j, i + j + k))
            k //= 2
        p *= 2
    return pairs


SUB = 8
NV = SEL_TOPK
NBLK = SUB * NV
assert NV == 16 and NBLK == LANES


def _topk_blocks_t(vt):
    v = [vt[SUB * a:SUB * (a + 1)] for a in range(NV)]
    s = list(v)
    for a, b in _sort16_network():
        s[a], s[b] = jnp.maximum(s[a], s[b]), jnp.minimum(s[a], s[b])
    for shift in (4, 2):
        s = [jnp.maximum(s[k], pltpu.roll(s[NV - 1 - k], shift, 0)) for k in range(NV)]
        d = NV // 2
        while d >= 1:
            for k in range(NV):
                if not k & d:
                    s[k], s[k + d] = jnp.maximum(s[k], s[k + d]), jnp.minimum(s[k], s[k + d])
            d //= 2
    top = [jnp.maximum(s[k], pltpu.roll(s[NV - 1 - k], 1, 0)) for k in range(NV)]
    thr = top[0]
    for k in range(1, NV):
        thr = jnp.minimum(thr, top[k])
    sub = lax.broadcasted_iota(jnp.int32, v[0].shape, 0)
    gt = [jnp.where(x > thr, 1.0, 0.0) for x in v]
    eq = [jnp.where(x == thr, 1.0, 0.0) for x in v]

    def sub_scan(x):
        for sh in (1, 2, 4):
            x = x + jnp.where(sub >= sh, pltpu.roll(x, sh, 0), 0.0)
        return x

    n_gt = gt[0]
    for a in range(1, NV):
        n_gt = n_gt + gt[a]
    n_gt = sub_scan(n_gt)
    need = float(SEL_TOPK) - jnp.broadcast_to(n_gt[SUB - 1:SUB], n_gt.shape)
    out, before = [], jnp.zeros_like(thr)
    for a in range(NV):
        inc = sub_scan(eq[a])
        out.append((gt[a] > 0.5) | ((eq[a] > 0.5) & (before + inc - eq[a] < need)))
        before = before + jnp.broadcast_to(inc[SUB - 1:SUB], inc.shape)
    return jnp.concatenate(out, axis=0)


def _nsa_body(q_ref, kc_ref, vc_ref, ks_ref, vs_ref, kw_ref, vw_ref, gate_ref, z_ref, o_ref,
              kaug_ref, vaug_ref, ov_ref):
    i = pl.program_id(1)
    L = ks_ref.shape[0]
    ncmp = kc_ref.shape[0]
    t0 = i * QB
    rows = B_REP * QB

    @pl.when(i == 0)
    def _():
        kaug_ref[:, 0:B_HD] = ks_ref[...]
        vaug_ref[:, 0:B_HD] = vs_ref[...]
        vaug_ref[:, B_HD:] = jnp.ones((L, B_HD), bf16)
        for c in range(L // KT):
            u = c * KT + lax.broadcasted_iota(jnp.int32, (KT, NBLK), 0)
            j = lax.broadcasted_iota(jnp.int32, (KT, NBLK), 1)
            kaug_ref[c * KT:(c + 1) * KT, B_HD:] = jnp.where(j == (u >> SEL_SHIFT), 1.0, 0.0).astype(bf16)
        ci = lax.broadcasted_iota(jnp.int32, (ncmp, NBLK), 0)
        sj = lax.broadcasted_iota(jnp.int32, (ncmp, NBLK), 1)
        ov_ref[...] = jnp.where((ci * CMP_STRIDE < (sj + 1) * SEL_LEN) & (ci * CMP_STRIDE + CMP_LEN > sj * SEL_LEN),
                                1.0, 0.0).astype(bf16)

    q = jnp.concatenate([q_ref[:, r * B_HD:(r + 1) * B_HD] for r in range(B_REP)], axis=0)
    q = (q.astype(f32) * (B_HD ** -0.5 * LOG2E)).astype(bf16)
    t_rows = t0 + (lax.broadcasted_iota(jnp.int32, (rows, 1), 0) & (QB - 1))
    t_tok = t0 + lax.broadcasted_iota(jnp.int32, (QB, 1), 0)

    def heads(x):
        return jnp.concatenate([x] * B_REP, axis=0)

    n_idx = lax.broadcasted_iota(jnp.int32, (QB, ncmp), 1)
    s_c = _dot_nt(q, kc_ref[...]) + heads(jnp.where(n_idx * CMP_STRIDE + (CMP_LEN - 1) <= t_tok, 0.0, NEG))
    e_c = jnp.exp2(s_c - jnp.max(s_c, axis=-1, keepdims=True))
    inv_c = jnp.where(t_rows >= CMP_LEN - 1, 1.0, 0.0) / jnp.sum(e_c, axis=-1, keepdims=True)
    p_c = e_c * inv_c
    o_c = _dot(p_c.astype(bf16), vc_ref[...])

    psum = p_c[0:QB]
    for r in range(1, B_REP):
        psum = psum + p_c[r * QB:(r + 1) * QB]
    overlap = ov_ref[...]
    imp = jnp.zeros((QB, NBLK), f32)
    rem = psum
    for _ in range(3):
        part = rem.astype(bf16)
        imp = imp + _dot(part, overlap)
        rem = rem - part.astype(f32)
    tq = t0 + lax.broadcasted_iota(jnp.int32, (QB, NBLK), 0)
    blk = lax.broadcasted_iota(jnp.int32, (QB, NBLK), 1)
    dist = (tq >> SEL_SHIFT) - blk
    forced = (blk == 0) | ((dist >= 0) & (dist < SEL_LOCAL))
    val = jnp.where(forced, jnp.inf, jnp.where(blk * SEL_LEN <= tq, imp, -jnp.inf))
    vt = val.T
    picked_t = _topk_blocks_t(vt) & (vt > -jnp.inf)

    td = pl.multiple_of(t0, QB)
    lane_d = lax.broadcasted_iota(jnp.int32, (QB, QB), 1)
    s_d = _dot_nt(q, ks_ref[pl.ds(td, QB), :]) + heads(jnp.where(t0 + lane_d <= t_tok, 0.0, NEG))
    m_d = jnp.max(s_d, axis=-1, keepdims=True)
    acc_d = _dot(jnp.exp2(s_d - m_d).astype(bf16), vaug_ref[pl.ds(td, QB), :])
    bsub = lax.broadcasted_iota(jnp.int32, (NBLK, QB), 0)
    bias = jnp.where(picked_t & (bsub * SEL_LEN < t0), 0.0, NEG).T.astype(bf16)
    qa = jnp.concatenate([q, jnp.concatenate([bias] * B_REP, axis=0)], axis=1)

    def sel_step(it, carry):
        m, acc = carry
        tiles = []
        for u in range(SEL_UNROLL):
            k0 = pl.multiple_of((it * SEL_UNROLL + u) * KT, KT)
            tiles.append((_dot_nt(qa, kaug_ref[pl.ds(k0, KT), :]), k0))
        for s, k0 in tiles:
            m_new = jnp.maximum(m, jnp.max(s, axis=-1, keepdims=True))
            p = jnp.exp2(s - m_new).astype(bf16)
            acc = jnp.exp2(m - m_new) * acc + _dot(p, vaug_ref[pl.ds(k0, KT), :])
            m = m_new
        return m, acc

    n_steps = (t0 + SEL_UNROLL * KT - 1) // (SEL_UNROLL * KT)
    _, acc_s = lax.fori_loop(0, n_steps, sel_step, (m_d, acc_d))
    o_s = acc_s[:, 0:B_HD] / acc_s[:, B_HD:]

    wlen = WINDOW + QB
    w0 = pl.multiple_of(jnp.maximum(t0 - WINDOW, 0), QB)
    kpos = w0 + lax.broadcasted_iota(jnp.int32, (QB, wlen), 1)
    s_w = _dot_nt(q, kw_ref[pl.ds(w0, wlen), :]) + heads(
        jnp.where((kpos <= t_tok) & (kpos > t_tok - WINDOW), 0.0, NEG))
    e_w = jnp.exp2(s_w - jnp.max(s_w, axis=-1, keepdims=True))
    o_w = _dot(e_w.astype(bf16), vw_ref[pl.ds(w0, wlen), :]) * (1.0 / jnp.sum(e_w, axis=-1, keepdims=True))

    gt = _sigmoid(gate_ref[...])
    for r in range(B_REP):
        rs = slice(r * QB, (r + 1) * QB)
        o = (gt[:, 3 * r:3 * r + 1] * o_c[rs] + gt[:, 3 * r + 1:3 * r + 2] * o_s[rs]
             + gt[:, 3 * r + 2:3 * r + 3] * o_w[rs])
        o_ref[:, r * B_HD:(r + 1) * B_HD] = (o * _silu(z_ref[:, r * B_HD:(r + 1) * B_HD])).astype(o_ref.dtype)


def nsa_mixer(p32, p16, cmp):
    L = p32.shape[0]
    assert L // SEL_LEN <= NBLK and L % (SEL_UNROLL * KT) == 0
    ncmp = cmp.shape[2]
    gw = B_REP * B_HD
    kv = lambda off: pl.BlockSpec((L, B_HD), lambda g, i: (0, off // B_HD + g))
    return pl.pallas_call(
        _nsa_body,
        grid=(B_KV, L // QB),
        in_specs=[pl.BlockSpec((QB, gw), lambda g, i: (i, H_QB // gw + g)),
                  pl.BlockSpec((None, None, ncmp, B_HD), lambda g, i: (0, g, 0, 0)),
                  pl.BlockSpec((None, None, ncmp, B_HD), lambda g, i: (1, g, 0, 0)),
                  kv(H_KS), kv(H_VS), kv(H_KW), kv(H_VW),
                  pl.BlockSpec((QB, LANES), lambda g, i: (i, C_GATES // LANES + g)),
                  pl.BlockSpec((QB, gw), lambda g, i: (i, C_ZB // gw + g))],
        out_specs=pl.BlockSpec((QB, gw), lambda g, i: (i, g)),
        out_shape=jax.ShapeDtypeStruct((L, WIDTH), bf16),
        scratch_shapes=[pltpu.VMEM((L, B_HD + NBLK), bf16), pltpu.VMEM((L, 2 * B_HD), bf16),
                        pltpu.VMEM((ncmp, NBLK), bf16)],
        compiler_params=_cparams("parallel", "arbitrary"),
        name="nsa_attention",
    )(p16, cmp, cmp, p16, p16, p16, p16, p32, p32)


def _merge_body(ya_ref, yb_ref, yc_ref, w_ref, ga_ref, gb_ref, gc_ref, o_ref):
    acc = _sigmoid(ga_ref[...]) * _dot(ya_ref[...], w_ref[0])
    acc = acc + _sigmoid(gb_ref[...]) * _dot(yb_ref[...], w_ref[1])
    acc = acc + _sigmoid(gc_ref[...]) * _dot(yc_ref[...], w_ref[2])
    o_ref[...] = acc.astype(o_ref.dtype)


def merge_branches(ya, yb, yc, w_branch, p32):
    L = ya.shape[0]
    tm, tn = min(512, L), 512
    yspec = pl.BlockSpec((tm, WIDTH), lambda j, i: (i, 0))
    gspec = lambda b: pl.BlockSpec((tm, tn), lambda j, i: (i, (C_GM + b * D_MODEL) // tn + j))
    return pl.pallas_call(
        _merge_body,
        grid=(D_MODEL // tn, L // tm),
        in_specs=[yspec, yspec, yspec,
                  pl.BlockSpec((3, WIDTH, tn), lambda j, i: (0, 0, j)),
                  gspec(0), gspec(1), gspec(2)],
        out_specs=pl.BlockSpec((tm, tn), lambda j, i: (i, j)),
        out_shape=jax.ShapeDtypeStruct((L, D_MODEL), bf16),
        compiler_params=_cparams("parallel", "parallel"),
        name="branch_merge",
    )(ya, yb, yc, w_branch.astype(bf16), p32, p32, p32)


def _rms(x):
    return x * lax.rsqrt(jnp.mean(x * x, axis=-1, keepdims=True) + EPS)


def _post_body(x_ref, m_ref, p_ref, wo_ref, wp_ref, wg_ref, npost_ref, nple_ref, o_ref):
    x1 = x_ref[...] + _rms(_dot(m_ref[...], wo_ref[...])) * npost_ref[...]
    e = _dot(p_ref[...].astype(bf16), wp_ref[...])
    g = _sigmoid(_dot(_rms(x1).astype(bf16), wg_ref[...]))
    o_ref[...] = x1 + _rms(g * e) * nple_ref[...]


def post_mixer(x, merged, p, w_out, ple_proj, ple_gate, norm_post, ple_norm):
    L = x.shape[0]
    tm = min(256, L)
    const = lambda shape: pl.BlockSpec(shape, lambda i: (0,) * len(shape))
    return pl.pallas_call(
        _post_body,
        grid=(L // tm,),
        in_specs=[pl.BlockSpec((tm, D_MODEL), lambda i: (i, 0)),
                  pl.BlockSpec((tm, D_MODEL), lambda i: (i, 0)),
                  pl.BlockSpec((tm, P_DIM), lambda i: (i, 0)),
                  const((D_MODEL, D_MODEL)), const((P_DIM, D_MODEL)), const((D_MODEL, D_MODEL)),
                  const((1, D_MODEL)), const((1, D_MODEL))],
        out_specs=pl.BlockSpec((tm, D_MODEL), lambda i: (i, 0)),
        out_shape=jax.ShapeDtypeStruct((L, D_MODEL), f32),
        compiler_params=_cparams("parallel"),
        name="post_mixer",
    )(x, merged, p, w_out.astype(bf16), ple_proj.astype(bf16), ple_gate.astype(bf16),
      norm_post.reshape(1, D_MODEL), ple_norm.reshape(1, D_MODEL))


def pack_w_in(w):
    offs = np.cumsum(IN_SIZES)[:-1].tolist()
    (xa, za, ba, ca, dta, qb, kcb, vcb, ksb, vsb, kwb, vwb, gb, zb, qc, kc, vc, zc, gm) = jnp.split(
        w.astype(bf16), offs, axis=1)
    k = w.shape[0]
    zeros = lambda n: jnp.zeros((k, n), bf16)
    ng = 3 * B_REP
    gates = [t for g in range(B_KV) for t in (gb[:, g * ng:(g + 1) * ng], zeros(LANES - ng))]
    w32 = jnp.concatenate([xa, ba, ca, za, qc, kc, zc, zb, gm, kcb, vcb, *gates, dta,
                           zeros(N32 - C_DT - A_HEADS)], axis=1)
    w16 = jnp.concatenate([qb, vc, ksb, vsb, kwb, vwb], axis=1)
    return w32, w16


def layer(x, p, norm_pre, norm_post, w_in, conv_w, conv_b, dt_bias, a_log, d_skip, ssm_norm,
          cmp_pe, cmp_w1, cmp_w2, ret_norm, w_branch, w_out, ple_proj, ple_gate, ple_norm):
    w32, w16 = pack_w_in(w_in)
    p32 = norm_matmul(x, norm_pre, w32, f32, "in_proj_f32")
    p16 = norm_matmul(x, norm_pre, w16, bf16, "in_proj_bf16")
    ya = ssd_mixer(p32, conv_w, conv_b, dt_bias, a_log, d_skip, ssm_norm)
    cmp = compress_kv(p32, cmp_pe, cmp_w1, cmp_w2)
    yb = nsa_mixer(p32, p16, cmp)
    yc = retention_mixer(p32, p16, ret_norm)
    merged = merge_branches(ya, yb, yc, w_branch, p32)
    return post_mixer(x, merged, p, w_out, ple_proj, ple_gate, norm_post, ple_norm)


def kernel(x, p, norm_pre, norm_post, w_in, conv_w, conv_b, dt_bias, a_log, d_skip, ssm_norm, cmp_pe, cmp_w1,
           cmp_w2, ret_norm, w_branch, w_out, ple_proj, ple_gate, ple_norm):
    b, L, d = x.shape
    assert b == 1 and d == D_MODEL
    xf = x.reshape(L, d)
    for i in range(p.shape[0]):
        xf = layer(xf, p[i, 0], norm_pre[i], norm_post[i], w_in[i], conv_w[i], conv_b[i], dt_bias[i], a_log[i],
                   d_skip[i], ssm_norm[i], cmp_pe[i], cmp_w1[i], cmp_w2[i], ret_norm[i], w_branch[i], w_out[i],
                   ple_proj[i], ple_gate[i], ple_norm[i])
    return xf.reshape(b, L, d)
```

```python
import functools

import jax
import jax.numpy as jnp
import numpy as np
from jax import lax
from jax.experimental import pallas as pl
from jax.experimental.pallas import tpu as pltpu

f32 = jnp.float32
bf16 = jnp.bfloat16

D_MODEL = 2048
P_DIM = 256
EPS = 1e-6
NEG = -1e30
WIDTH = 2048

A_HEADS, A_HEAD_DIM, A_GROUPS, A_STATE, A_CONV, CHUNK = 32, 64, 8, 128, 4, 128
A_GW = (A_HEADS // A_GROUPS) * A_HEAD_DIM
A_CONV_CH = WIDTH + 2 * A_GROUPS * A_STATE
B_HEADS, B_KV, B_HD = 16, 4, 128
B_REP = B_HEADS // B_KV
CMP_LEN, CMP_STRIDE, CMP_HIDDEN = 32, 16, 256
SEL_LEN, SEL_TOPK, SEL_LOCAL, WINDOW, QB = 64, 16, 2, 512, 128
SEL_SHIFT = SEL_LEN.bit_length() - 1
assert SEL_LOCAL * SEL_LEN >= QB and QB % SEL_LEN == 0
LOG2E = 1.4426950408889634
KT = 1024
SEL_UNROLL = 2
C_HEADS, C_HD = 8, 256
ROPE_BASE = 10000.0

IN_SIZES = (2048, 2048, 1024, 1024, 32,
            2048, 512, 512, 512, 512, 512, 512, 48, 2048,
            2048, 2048, 2048, 2048, 6144)

C_XBC, C_ZA, C_QC, C_KC, C_ZC, C_ZB, C_GM = 0, 4096, 6144, 8192, 10240, 12288, 14336
C_KCB, C_VCB, C_GATES, C_DT, N32 = 20480, 20992, 21504, 22016, 22528
H_QB, H_VC, H_KS, H_VS, H_KW, H_VW, N16 = 0, 2048, 4096, 4608, 5120, 5632, 6144

LANES = 128
VMEM_LIMIT = 56 * 1024 * 1024


def _cparams(*sem):
    return pltpu.CompilerParams(dimension_semantics=sem, vmem_limit_bytes=VMEM_LIMIT)


def _sigmoid(x):
    return 1.0 / (1.0 + jnp.exp(-x))


def _silu(x):
    return x * _sigmoid(x)


def _dot(a, b):
    return jnp.dot(a, b, preferred_element_type=f32)


def _dot_nt(a, b):
    return lax.dot_general(a, b, (((1,), (1,)), ((), ())), preferred_element_type=f32)


def _dot_tn(a, b):
    return lax.dot_general(a, b, (((0,), (0,)), ((), ())), preferred_element_type=f32)


def _norm_matmul_body(x_ref, g_ref, w_ref, o_ref, h_ref):
    @pl.when(pl.program_id(1) == 0)
    def _():
        xf = x_ref[...]
        ms = jnp.mean(xf * xf, axis=-1, keepdims=True)
        h_ref[...] = (xf * lax.rsqrt(ms + EPS) * g_ref[...]).astype(bf16)

    o_ref[...] = _dot(h_ref[...], w_ref[...]).astype(o_ref.dtype)


def norm_matmul(x, gain, w, out_dtype, name):
    m, k = x.shape
    n = w.shape[1]
    tm, tn = min(1024, m), 1024
    return pl.pallas_call(
        _norm_matmul_body,
        grid=(m // tm, n // tn),
        in_specs=[pl.BlockSpec((tm, k), lambda i, j: (i, 0)),
                  pl.BlockSpec((1, k), lambda i, j: (0, 0)),
                  pl.BlockSpec((k, tn), lambda i, j: (0, j))],
        out_specs=pl.BlockSpec((tm, tn), lambda i, j: (i, j)),
        out_shape=jax.ShapeDtypeStruct((m, n), out_dtype),
        scratch_shapes=[pltpu.VMEM((tm, k), bf16)],
        compiler_params=_cparams("parallel", "arbitrary"),
        name=name,
    )(x, gain.reshape(1, k), w)


def _expand_heads(v, g, width):
    rows = v.shape[0]
    hd = width // 4
    lane = lax.broadcasted_iota(jnp.int32, (rows, width), 1)
    out = jnp.broadcast_to(v[:, 4 * g + 3:4 * g + 4], (rows, width))
    for r in (2, 1, 0):
        out = jnp.where(lane < (r + 1) * hd, jnp.broadcast_to(v[:, 4 * g + r:4 * g + r + 1], (rows, width)), out)
    return out


def _ssd_body(xbc_ref, z_ref, dt_ref, cw_ref, cb_ref, dtb_ref, alog_ref, dskip_ref, nw_ref,
              o_ref, tail_ref, st_ref, y_ref):
    c = pl.program_id(0)

    @pl.when(c == 0)
    def _():
        tail_ref[...] = jnp.zeros_like(tail_ref)
        st_ref[...] = jnp.zeros_like(st_ref)

    cur = xbc_ref[...]
    tail = tail_ref[...]
    row8 = lax.broadcasted_iota(jnp.int32, (8, A_CONV_CH), 0)
    acc = cur * cw_ref[A_CONV - 1:A_CONV, :] + cb_ref[...]
    for s in range(1, A_CONV):
        rolled = pltpu.roll(cur, s, 0)
        first = jnp.where(row8 < s, pltpu.roll(tail, s, 0), rolled[0:8])
        shifted = jnp.concatenate([first, rolled[8:]], axis=0)
        acc = acc + shifted * cw_ref[A_CONV - 1 - s:A_CONV - s, :]
    tail_ref[...] = cur[CHUNK - 8:CHUNK]
    xbc = _silu(acc)
    xs = xbc[:, 0:WIDTH]
    bm = xbc[:, WIDTH:WIDTH + A_GROUPS * A_STATE]
    cm = xbc[:, WIDTH + A_GROUPS * A_STATE:]

    xdt = dt_ref[...] + dtb_ref[...]
    dt = jnp.maximum(xdt, 0.0) + jnp.log1p(jnp.exp(-jnp.abs(xdt)))
    a = -jnp.exp(alog_ref[...]) * dt
    rowi = lax.broadcasted_iota(jnp.int32, (CHUNK, LANES), 0)
    a_cs = a
    s = 1
    while s < CHUNK:
        a_cs = a_cs + jnp.where(rowi >= s, pltpu.roll(a_cs, s, 0), 0.0)
        s *= 2
    a_cs_t = a_cs.T
    a_last = a_cs[CHUNK - 1:CHUNK, :]
    exp_acs = jnp.exp(a_cs)
    dend = jnp.exp(a_last - a_cs)
    cdecay = jnp.exp(a_last)

    li = lax.broadcasted_iota(jnp.int32, (CHUNK, CHUNK), 0)
    si = lax.broadcasted_iota(jnp.int32, (CHUNK, CHUNK), 1)
    causal = li >= si
    lane_g = lax.broadcasted_iota(jnp.int32, (CHUNK, A_GW), 1)

    for g in range(A_GROUPS):
        cm_g = cm[:, g * A_STATE:(g + 1) * A_STATE].astype(bf16)
        bm_g = bm[:, g * A_STATE:(g + 1) * A_STATE].astype(bf16)
        xs_g = xs[:, g * A_GW:(g + 1) * A_GW]
        cb = _dot_nt(cm_g, bm_g)
        xdt_g = xs_g * _expand_heads(dt, g, A_GW)
        xdt_b = xdt_g.astype(bf16)
        lhs, rhs = [], []
        for r in range(4):
            h = 4 * g + r
            seg = a_cs[:, h:h + 1] - a_cs_t[h:h + 1, :]
            dec = jnp.exp(jnp.where(causal, seg, -jnp.inf))
            lhs.append((cb * dec).astype(bf16))
            keep = (lane_g >= r * A_HEAD_DIM) & (lane_g < (r + 1) * A_HEAD_DIM)
            rhs.append(jnp.where(keep, xdt_b, jnp.zeros_like(xdt_b)))
        y_diag = _dot(jnp.concatenate(lhs, axis=1), jnp.concatenate(rhs, axis=0))
        st = st_ref[g]
        y_off = _dot(cm_g, st.astype(bf16)) * _expand_heads(exp_acs, g, A_GW)
        y_ref[:, g * A_GW:(g + 1) * A_GW] = y_diag + y_off + xs_g * dskip_ref[:, g * A_GW:(g + 1) * A_GW]
        wgt = (xdt_g * _expand_heads(dend, g, A_GW)).astype(bf16)
        st_ref[g] = st * _expand_heads(cdecay, g, A_GW) + _dot_tn(bm_g, wgt)

    yz = y_ref[...] * _silu(z_ref[...])
    ms = jnp.mean(yz * yz, axis=-1, keepdims=True)
    o_ref[...] = (yz * lax.rsqrt(ms + EPS) * nw_ref[...]).astype(o_ref.dtype)


def ssd_mixer(p32, conv_w, conv_b, dt_bias, a_log, d_skip, norm_w):
    L = p32.shape[0]
    pad = LANES - A_HEADS
    dtb = jnp.pad(dt_bias, (0, pad)).reshape(1, LANES)
    alog = jnp.pad(a_log, (0, pad)).reshape(1, LANES)
    dskip = jnp.repeat(d_skip, A_HEAD_DIM).reshape(1, WIDTH)
    const = lambda shape: pl.BlockSpec(shape, lambda c: (0,) * len(shape))
    return pl.pallas_call(
        _ssd_body,
        grid=(L // CHUNK,),
        in_specs=[pl.BlockSpec((CHUNK, A_CONV_CH), lambda c: (c, C_XBC // A_CONV_CH)),
                  pl.BlockSpec((CHUNK, WIDTH), lambda c: (c, C_ZA // WIDTH)),
                  pl.BlockSpec((CHUNK, LANES), lambda c: (c, C_DT // LANES)),
                  const((A_CONV, A_CONV_CH)), const((1, A_CONV_CH)), const((1, LANES)), const((1, LANES)),
                  const((1, WIDTH)), const((1, WIDTH))],
        out_specs=pl.BlockSpec((CHUNK, WIDTH), lambda c: (c, 0)),
        out_shape=jax.ShapeDtypeStruct((L, WIDTH), bf16),
        scratch_shapes=[pltpu.VMEM((8, A_CONV_CH), f32),
                        pltpu.VMEM((A_GROUPS, A_STATE, A_GW), f32),
                        pltpu.VMEM((CHUNK, WIDTH), f32)],
        compiler_params=_cparams("arbitrary"),
        name="ssd_mixer",
    )(p32, p32, p32, conv_w, conv_b.reshape(1, A_CONV_CH), dtb, alog, dskip, norm_w.reshape(1, WIDTH))


def _ret_body(q_ref, k_ref, z_ref, v_ref, cos_ref, sin_ref, dmat_ref, zeta_ref, xi_ref, cd_ref, gn_ref,
              o_ref, r_ref):
    c = pl.program_id(0)

    @pl.when(c == 0)
    def _():
        r_ref[...] = jnp.zeros_like(r_ref)

    cos = jnp.concatenate([cos_ref[...]] * C_HEADS, axis=1)
    sin = jnp.concatenate([sin_ref[...]] * C_HEADS, axis=1)
    even = (lax.broadcasted_iota(jnp.int32, (CHUNK, WIDTH), 1) & 1) == 0

    def rot(x):
        cols = [x[:, j * LANES:(j + 1) * LANES] for j in range(WIDTH // LANES)]
        right = jnp.concatenate([pltpu.roll(c_, LANES - 1, 1) for c_ in cols], axis=1)
        left = jnp.concatenate([pltpu.roll(c_, 1, 1) for c_ in cols], axis=1)
        return x * cos + jnp.where(even, right, left) * sin

    q_rot = rot(q_ref[...])
    k_rot = rot(k_ref[...]) * (C_HD ** -0.5)

    for h in range(C_HEADS):
        sl = slice(h * C_HD, (h + 1) * C_HD)
        qr = q_rot[:, sl]
        kr = k_rot[:, sl]
        qb, kb = qr.astype(bf16), kr.astype(bf16)
        v = v_ref[:, sl]
        s = _dot_nt(qb, kb) * dmat_ref[h]
        inner = _dot(s.astype(bf16), v)
        rst = r_ref[h]
        cross = _dot(qb, rst.astype(bf16)) * xi_ref[:, h:h + 1]
        o = inner + cross
        mu = jnp.mean(o, axis=-1, keepdims=True)
        d = o - mu
        var = jnp.mean(d * d, axis=-1, keepdims=True)
        on = d * lax.rsqrt(var + EPS) * gn_ref[:, sl]
        o_ref[:, sl] = (_silu(z_ref[:, sl]) * on).astype(o_ref.dtype)
        kz = (kr * zeta_ref[:, h:h + 1]).astype(bf16)
        r_ref[h] = rst * cd_ref[0:1, h:h + 1] + _dot_tn(kz, v)


def _retention_tables(L):
    T, H, d = CHUNK, C_HEADS, C_HD
    pos = jnp.arange(L)
    inv = ROPE_BASE ** (-jnp.arange(0, d, 2, dtype=f32) / d)
    ang = pos.astype(f32)[:, None] * inv[None, :]
    log_g = jnp.log1p(-jnp.exp2(-5.0 - jnp.arange(H, dtype=f32)))
    i = jnp.arange(T, dtype=f32)
    diff = i[:, None] - i[None, :]
    dmat = jnp.where(diff >= 0, jnp.exp(log_g[:, None, None] * jnp.maximum(diff, 0.0)), 0.0)
    zeta = jnp.exp(log_g[:, None] * (T - 1 - i)[None, :])
    xi = jnp.exp(log_g[:, None] * (i + 1.0)[None, :])
    cdecay = jnp.exp(log_g * T)
    padl = lambda t: jnp.pad(t, ((0, 0), (0, LANES - H)))
    cos, sin = jnp.cos(ang), jnp.sin(ang)
    cos2 = jnp.stack([cos, cos], axis=-1).reshape(L, d)
    sin2 = jnp.stack([-sin, sin], axis=-1).reshape(L, d)
    return (cos2, sin2, dmat, padl(zeta.T), padl(xi.T), padl(cdecay[None, :]))


def retention_mixer(p32, p16, gn_w):
    L = p32.shape[0]
    cos, sin, dmat, zeta_t, xi_t, cdecay = _retention_tables(L)
    const = lambda shape: pl.BlockSpec(shape, lambda c: (0,) * len(shape))
    return pl.pallas_call(
        _ret_body,
        grid=(L // CHUNK,),
        in_specs=[pl.BlockSpec((CHUNK, WIDTH), lambda c: (c, C_QC // WIDTH)),
                  pl.BlockSpec((CHUNK, WIDTH), lambda c: (c, C_KC // WIDTH)),
                  pl.BlockSpec((CHUNK, WIDTH), lambda c: (c, C_ZC // WIDTH)),
                  pl.BlockSpec((CHUNK, WIDTH), lambda c: (c, H_VC // WIDTH)),
                  pl.BlockSpec((CHUNK, C_HD), lambda c: (c, 0)),
                  pl.BlockSpec((CHUNK, C_HD), lambda c: (c, 0)),
                  const((C_HEADS, CHUNK, CHUNK)), const((CHUNK, LANES)), const((CHUNK, LANES)),
                  const((1, LANES)), const((1, WIDTH))],
        out_specs=pl.BlockSpec((CHUNK, WIDTH), lambda c: (c, 0)),
        out_shape=jax.ShapeDtypeStruct((L, WIDTH), bf16),
        scratch_shapes=[pltpu.VMEM((C_HEADS, C_HD, C_HD), f32)],
        compiler_params=_cparams("arbitrary"),
        name="retention_mixer",
    )(p32, p32, p32, p16, cos, sin, dmat, zeta_t, xi_t, cdecay, gn_w.reshape(1, WIDTH))


def _compress_body(x_ref, pe_ref, w1_ref, w2_ref, o_ref):
    n = o_ref.shape[0]
    acc = [jnp.zeros((n, CMP_HIDDEN), f32) for _ in range(2)]
    for b in range(CMP_STRIDE):
        x = x_ref[pl.ds(b, n, stride=CMP_STRIDE), :]
        for a in range(2):
            acc[a] = acc[a] + _dot((x + pe_ref[a, b]).astype(bf16), w1_ref[a, b])
    pre = acc[0] + pltpu.roll(acc[1], n - 1, 0)
    out = _dot(_silu(pre).astype(bf16), w2_ref[...])
    row = lax.broadcasted_iota(jnp.int32, out.shape, 0)
    o_ref[...] = jnp.where(row < n - 1, out, 0.0).astype(o_ref.dtype)


def compress_kv(p32, cmp_pe, cmp_w1, cmp_w2):
    L = p32.shape[0]
    nblk = L // CMP_STRIDE
    w1 = cmp_w1.reshape(2, 2, CMP_STRIDE, B_HD, CMP_HIDDEN).astype(bf16)
    pe = cmp_pe.reshape(2, 2, CMP_STRIDE, 1, B_HD)
    w2 = cmp_w2.astype(bf16)
    return pl.pallas_call(
        _compress_body,
        grid=(2, B_KV),
        in_specs=[pl.BlockSpec((L, LANES), lambda w, h: (0, C_KCB // LANES + w * B_KV + h)),
                  pl.BlockSpec((None, 2, CMP_STRIDE, 1, B_HD), lambda w, h: (w, 0, 0, 0, 0)),
                  pl.BlockSpec((None, 2, CMP_STRIDE, B_HD, CMP_HIDDEN), lambda w, h: (w, 0, 0, 0, 0)),
                  pl.BlockSpec((None, CMP_HIDDEN, B_HD), lambda w, h: (w, 0, 0))],
        out_specs=pl.BlockSpec((None, None, nblk, B_HD), lambda w, h: (w, h, 0, 0)),
        out_shape=jax.ShapeDtypeStruct((2, B_KV, nblk, B_HD), bf16),
        compiler_params=_cparams("parallel", "parallel"),
        name="nsa_compress",
    )(p32, pe, w1, w2)


def _sort16_network():
    n, pairs, p = 16, [], 1
    while p < n:
        k = p
        while k >= 1:
            for j in range(k % p, n - k, 2 * k):
                for i in range(min(k, n - j - k)):
                    if (i + j) // (2 * p) == (i + j + k) // (2 * p):
                        pairs.append((i + j, i + j + k))
            k //= 2
        p *= 2
    return pairs


SUB = 8
NV = SEL_TOPK
NBLK = SUB * NV
assert NV == 16 and NBLK == LANES


def _topk_blocks_t(vt):
    v = [vt[SUB * a:SUB * (a + 1)] for a in range(NV)]
    s = list(v)
    for a, b in _sort16_network():
        s[a], s[b] = jnp.maximum(s[a], s[b]), jnp.minimum(s[a], s[b])
    for shift in (4, 2):
        s = [jnp.maximum(s[k], pltpu.roll(s[NV - 1 - k], shift, 0)) for k in range(NV)]
        d = NV // 2
        while d >= 1:
            for k in range(NV):
                if not k & d:
                    s[k], s[k + d] = jnp.maximum(s[k], s[k + d]), jnp.minimum(s[k], s[k + d])
            d //= 2
    top = [jnp.maximum(s[k], pltpu.roll(s[NV - 1 - k], 1, 0)) for k in range(NV)]
    thr = top[0]
    for k in range(1, NV):
        thr = jnp.minimum(thr, top[k])
    sub = lax.broadcasted_iota(jnp.int32, v[0].shape, 0)
    gt = [jnp.where(x > thr, 1.0, 0.0) for x in v]
    eq = [jnp.where(x == thr, 1.0, 0.0) for x in v]

    def sub_scan(x):
        for sh in (1, 2, 4):
            x = x + jnp.where(sub >= sh, pltpu.roll(x, sh, 0), 0.0)
        return x

    n_gt = gt[0]
    for a in range(1, NV):
        n_gt = n_gt + gt[a]
    n_gt = sub_scan(n_gt)
    need = float(SEL_TOPK) - jnp.broadcast_to(n_gt[SUB - 1:SUB], n_gt.shape)
    out, before = [], jnp.zeros_like(thr)
    for a in range(NV):
        inc = sub_scan(eq[a])
        out.append((gt[a] > 0.5) | ((eq[a] > 0.5) & (before + inc - eq[a] < need)))
        before = before + jnp.broadcast_to(inc[SUB - 1:SUB], inc.shape)
    return jnp.concatenate(out, axis=0)


def _nsa_body(q_ref, kc_ref, vc_ref, ks_ref, vs_ref, kw_ref, vw_ref, gate_ref, z_ref, o_ref,
              kaug_ref, vaug_ref, ov_ref):
    i = pl.program_id(1)
    L = ks_ref.shape[0]
    ncmp = kc_ref.shape[0]
    t0 = i * QB
    rows = B_REP * QB

    @pl.when(i == 0)
    def _():
        kaug_ref[:, 0:B_HD] = ks_ref[...]
        vaug_ref[:, 0:B_HD] = vs_ref[...]
        vaug_ref[:, B_HD:] = jnp.ones((L, B_HD), bf16)
        for c in range(L // KT):
            u = c * KT + lax.broadcasted_iota(jnp.int32, (KT, NBLK), 0)
            j = lax.broadcasted_iota(jnp.int32, (KT, NBLK), 1)
            kaug_ref[c * KT:(c + 1) * KT, B_HD:] = jnp.where(j == (u >> SEL_SHIFT), 1.0, 0.0).astype(bf16)
        ci = lax.broadcasted_iota(jnp.int32, (ncmp, NBLK), 0)
        sj = lax.broadcasted_iota(jnp.int32, (ncmp, NBLK), 1)
        ov_ref[...] = jnp.where((ci * CMP_STRIDE < (sj + 1) * SEL_LEN) & (ci * CMP_STRIDE + CMP_LEN > sj * SEL_LEN),
                                1.0, 0.0).astype(bf16)

    q = jnp.concatenate([q_ref[:, r * B_HD:(r + 1) * B_HD] for r in range(B_REP)], axis=0)
    q = (q.astype(f32) * (B_HD ** -0.5 * LOG2E)).astype(bf16)
    t_rows = t0 + (lax.broadcasted_iota(jnp.int32, (rows, 1), 0) & (QB - 1))
    t_tok = t0 + lax.broadcasted_iota(jnp.int32, (QB, 1), 0)

    def heads(x):
        return jnp.concatenate([x] * B_REP, axis=0)

    n_idx = lax.broadcasted_iota(jnp.int32, (QB, ncmp), 1)
    s_c = _dot_nt(q, kc_ref[...]) + heads(jnp.where(n_idx * CMP_STRIDE + (CMP_LEN - 1) <= t_tok, 0.0, NEG))
    e_c = jnp.exp2(s_c - jnp.max(s_c, axis=-1, keepdims=True))
    inv_c = jnp.where(t_rows >= CMP_LEN - 1, 1.0, 0.0) / jnp.sum(e_c, axis=-1, keepdims=True)
    p_c = e_c * inv_c
    o_c = _dot(p_c.astype(bf16), vc_ref[...])

    psum = p_c[0:QB]
    for r in range(1, B_REP):
        psum = psum + p_c[r * QB:(r + 1) * QB]
    overlap = ov_ref[...]
    imp = jnp.zeros((QB, NBLK), f32)
    rem = psum
    for _ in range(3):
        part = rem.astype(bf16)
        imp = imp + _dot(part, overlap)
        rem = rem - part.astype(f32)
    tq = t0 + lax.broadcasted_iota(jnp.int32, (QB, NBLK), 0)
    blk = lax.broadcasted_iota(jnp.int32, (QB, NBLK), 1)
    dist = (tq >> SEL_SHIFT) - blk
    forced = (blk == 0) | ((dist >= 0) & (dist < SEL_LOCAL))
    val = jnp.where(forced, jnp.inf, jnp.where(blk * SEL_LEN <= tq, imp, -jnp.inf))
    vt = val.T
    picked_t = _topk_blocks_t(vt) & (vt > -jnp.inf)

    td = pl.multiple_of(t0, QB)
    lane_d = lax.broadcasted_iota(jnp.int32, (QB, QB), 1)
    s_d = _dot_nt(q, ks_ref[pl.ds(td, QB), :]) + heads(jnp.where(t0 + lane_d <= t_tok, 0.0, NEG))
    m_d = jnp.max(s_d, axis=-1, keepdims=True)
    acc_d = _dot(jnp.exp2(s_d - m_d).astype(bf16), vaug_ref[pl.ds(td, QB), :])
    bsub = lax.broadcasted_iota(jnp.int32, (NBLK, QB), 0)
    bias = jnp.where(picked_t & (bsub * SEL_LEN < t0), 0.0, NEG).T.astype(bf16)
    qa = jnp.concatenate([q, jnp.concatenate([bias] * B_REP, axis=0)], axis=1)

    def sel_step(it, carry):
        m, acc = carry
        tiles = []
        for u in range(SEL_UNROLL):
            k0 = pl.multiple_of((it * SEL_UNROLL + u) * KT, KT)
            tiles.append((_dot_nt(qa, kaug_ref[pl.ds(k0, KT), :]), k0))
        for s, k0 in tiles:
            m_new = jnp.maximum(m, jnp.max(s, axis=-1, keepdims=True))
            p = jnp.exp2(s - m_new).astype(bf16)
            acc = jnp.exp2(m - m_new) * acc + _dot(p, vaug_ref[pl.ds(k0, KT), :])
            m = m_new
        return m, acc

    n_steps = (t0 + SEL_UNROLL * KT - 1) // (SEL_UNROLL * KT)
    _, acc_s = lax.fori_loop(0, n_steps, sel_step, (m_d, acc_d))
    o_s = acc_s[:, 0:B_HD] / acc_s[:, B_HD:]

    wlen = WINDOW + QB
    w0 = pl.multiple_of(jnp.maximum(t0 - WINDOW, 0), QB)
    kpos = w0 + lax.broadcasted_iota(jnp.int32, (QB, wlen), 1)
    s_w = _dot_nt(q, kw_ref[pl.ds(w0, wlen), :]) + heads(
        jnp.where((kpos <= t_tok) & (kpos > t_tok - WINDOW), 0.0, NEG))
    e_w = jnp.exp2(s_w - jnp.max(s_w, axis=-1, keepdims=True))
    o_w = _dot(e_w.astype(bf16), vw_ref[pl.ds(w0, wlen), :]) * (1.0 / jnp.sum(e_w, axis=-1, keepdims=True))

    ng = 3 * B_REP
    gt = _sigmoid(pltpu.roll(gate_ref[...], (LANES - ng * pl.program_id(0)) % LANES, 1))
    for r in range(B_REP):
        rs = slice(r * QB, (r + 1) * QB)
        o = (gt[:, 3 * r:3 * r + 1] * o_c[rs] + gt[:, 3 * r + 1:3 * r + 2] * o_s[rs]
             + gt[:, 3 * r + 2:3 * r + 3] * o_w[rs])
        o_ref[:, r * B_HD:(r + 1) * B_HD] = (o * _silu(z_ref[:, r * B_HD:(r + 1) * B_HD])).astype(o_ref.dtype)


def nsa_mixer(p32, p16, cmp):
    L = p32.shape[0]
    assert L // SEL_LEN <= NBLK and L % (SEL_UNROLL * KT) == 0
    ncmp = cmp.shape[2]
    gw = B_REP * B_HD
    kv = lambda off: pl.BlockSpec((L, B_HD), lambda g, i: (0, off // B_HD + g))
    return pl.pallas_call(
        _nsa_body,
        grid=(B_KV, L // QB),
        in_specs=[pl.BlockSpec((QB, gw), lambda g, i: (i, H_QB // gw + g)),
                  pl.BlockSpec((None, None, ncmp, B_HD), lambda g, i: (0, g, 0, 0)),
                  pl.BlockSpec((None, None, ncmp, B_HD), lambda g, i: (1, g, 0, 0)),
                  kv(H_KS), kv(H_VS), kv(H_KW), kv(H_VW),
                  pl.BlockSpec((QB, LANES), lambda g, i: (i, C_GATES // LANES)),
                  pl.BlockSpec((QB, gw), lambda g, i: (i, C_ZB // gw + g))],
        out_specs=pl.BlockSpec((QB, gw), lambda g, i: (i, g)),
        out_shape=jax.ShapeDtypeStruct((L, WIDTH), bf16),
        scratch_shapes=[pltpu.VMEM((L, B_HD + NBLK), bf16), pltpu.VMEM((L, 2 * B_HD), bf16),
                        pltpu.VMEM((ncmp, NBLK), bf16)],
        compiler_params=_cparams("parallel", "arbitrary"),
        name="nsa_attention",
    )(p16, cmp, cmp, p16, p16, p16, p16, p32, p32)


def _merge_body(ya_ref, yb_ref, yc_ref, w_ref, ga_ref, gb_ref, gc_ref, o_ref):
    acc = _sigmoid(ga_ref[...]) * _dot(ya_ref[...], w_ref[0])
    acc = acc + _sigmoid(gb_ref[...]) * _dot(yb_ref[...], w_ref[1])
    acc = acc + _sigmoid(gc_ref[...]) * _dot(yc_ref[...], w_ref[2])
    o_ref[...] = acc.astype(o_ref.dtype)


def merge_branches(ya, yb, yc, w_branch, p32):
    L = ya.shape[0]
    tm, tn = min(512, L), 512
    yspec = pl.BlockSpec((tm, WIDTH), lambda j, i: (i, 0))
    gspec = lambda b: pl.BlockSpec((tm, tn), lambda j, i: (i, (C_GM + b * D_MODEL) // tn + j))
    return pl.pallas_call(
        _merge_body,
        grid=(D_MODEL // tn, L // tm),
        in_specs=[yspec, yspec, yspec,
                  pl.BlockSpec((3, WIDTH, tn), lambda j, i: (0, 0, j)),
                  gspec(0), gspec(1), gspec(2)],
        out_specs=pl.BlockSpec((tm, tn), lambda j, i: (i, j)),
        out_shape=jax.ShapeDtypeStruct((L, D_MODEL), bf16),
        compiler_params=_cparams("parallel", "parallel"),
        name="branch_merge",
    )(ya, yb, yc, w_branch.astype(bf16), p32, p32, p32)


def _rms(x):
    return x * lax.rsqrt(jnp.mean(x * x, axis=-1, keepdims=True) + EPS)


def _post_body(x_ref, m_ref, p_ref, wo_ref, wp_ref, wg_ref, npost_ref, nple_ref, o_ref):
    x1 = x_ref[...] + _rms(_dot(m_ref[...], wo_ref[...])) * npost_ref[...]
    e = _dot(p_ref[...].astype(bf16), wp_ref[...])
    g = _sigmoid(_dot(_rms(x1).astype(bf16), wg_ref[...]))
    o_ref[...] = x1 + _rms(g * e) * nple_ref[...]


def post_mixer(x, merged, p, w_out, ple_proj, ple_gate, norm_post, ple_norm):
    L = x.shape[0]
    tm = min(256, L)
    const = lambda shape: pl.BlockSpec(shape, lambda i: (0,) * len(shape))
    return pl.pallas_call(
        _post_body,
        grid=(L // tm,),
        in_specs=[pl.BlockSpec((tm, D_MODEL), lambda i: (i, 0)),
                  pl.BlockSpec((tm, D_MODEL), lambda i: (i, 0)),
                  pl.BlockSpec((tm, P_DIM), lambda i: (i, 0)),
                  const((D_MODEL, D_MODEL)), const((P_DIM, D_MODEL)), const((D_MODEL, D_MODEL)),
                  const((1, D_MODEL)), const((1, D_MODEL))],
        out_specs=pl.BlockSpec((tm, D_MODEL), lambda i: (i, 0)),
        out_shape=jax.ShapeDtypeStruct((L, D_MODEL), f32),
        compiler_params=_cparams("parallel"),
        name="post_mixer",
    )(x, merged, p, w_out.astype(bf16), ple_proj.astype(bf16), ple_gate.astype(bf16),
      norm_post.reshape(1, D_MODEL), ple_norm.reshape(1, D_MODEL))


PACK_T = 512
PACK_WIN = PACK_T // LANES + 1


def _pack_sources():
    names = ("xa", "za", "ba", "ca", "dta", "qb", "kcb", "vcb", "ksb", "vsb", "kwb", "vwb", "gb", "zb",
             "qc", "kc", "vc", "zc", "gm")
    off = dict(zip(names, [0] + np.cumsum(IN_SIZES)[:-1].tolist()))
    size = dict(zip(names, IN_SIZES))
    tiles = lambda *sec: [off[s] + PACK_T * t for s in sec for t in range(size[s] // PACK_T)]
    src32 = tiles("xa", "ba", "ca", "za", "qc", "kc", "zc", "zb", "gm", "kcb", "vcb") + [off["gb"], off["dta"]]
    src16 = tiles("qb", "vc", "ksb", "vsb", "kwb", "vwb")
    assert len(src32) * PACK_T == N32 and len(src16) * PACK_T == N16
    return src32, src16


def _pack_body(blk_ref, sh_ref, *refs):
    o_ref = refs[-1]
    win = jnp.concatenate([r[...] for r in refs[:-1]], axis=1)
    shift = sh_ref[pl.program_id(0)]
    width = PACK_WIN * LANES
    o_ref[...] = pltpu.roll(win, (width - shift) % width, 1)[:, :PACK_T].astype(o_ref.dtype)


def pack_w_in(w_in, layer_idx, src):
    k = w_in.shape[1]
    src = np.asarray(src, np.int32)
    blk, shift = jnp.asarray(src // LANES), jnp.asarray(src % LANES)
    window = lambda n: pl.BlockSpec((None, k, LANES), lambda j, blk, sh: (layer_idx, 0, blk[j] + n))
    grid_spec = pltpu.PrefetchScalarGridSpec(
        num_scalar_prefetch=2, grid=(len(src),),
        in_specs=[window(n) for n in range(PACK_WIN)],
        out_specs=pl.BlockSpec((k, PACK_T), lambda j, blk, sh: (0, j)))
    return pl.pallas_call(
        _pack_body, grid_spec=grid_spec,
        out_shape=jax.ShapeDtypeStruct((k, len(src) * PACK_T), bf16),
        compiler_params=_cparams("parallel"),
        name="pack_w_in",
    )(blk, shift, *([w_in] * PACK_WIN))


def layer(i, x, p, norm_pre, norm_post, w_in, conv_w, conv_b, dt_bias, a_log, d_skip, ssm_norm,
          cmp_pe, cmp_w1, cmp_w2, ret_norm, w_branch, w_out, ple_proj, ple_gate, ple_norm):
    src32, src16 = _pack_sources()
    w32, w16 = pack_w_in(w_in, i, src32), pack_w_in(w_in, i, src16)
    p32 = norm_matmul(x, norm_pre, w32, f32, "in_proj_f32")
    p16 = norm_matmul(x, norm_pre, w16, bf16, "in_proj_bf16")
    ya = ssd_mixer(p32, conv_w, conv_b, dt_bias, a_log, d_skip, ssm_norm)
    cmp = compress_kv(p32, cmp_pe, cmp_w1, cmp_w2)
    yb = nsa_mixer(p32, p16, cmp)
    yc = retention_mixer(p32, p16, ret_norm)
    merged = merge_branches(ya, yb, yc, w_branch, p32)
    return post_mixer(x, merged, p, w_out, ple_proj, ple_gate, norm_post, ple_norm)


def kernel(x, p, norm_pre, norm_post, w_in, conv_w, conv_b, dt_bias, a_log, d_skip, ssm_norm, cmp_pe, cmp_w1,
           cmp_w2, ret_norm, w_branch, w_out, ple_proj, ple_gate, ple_norm):
    b, L, d = x.shape
    assert b == 1 and d == D_MODEL
    xf = x.reshape(L, d)
    for i in range(p.shape[0]):
        xf = layer(i, xf, p[i, 0], norm_pre[i], norm_post[i], w_in, conv_w[i], conv_b[i], dt_bias[i], a_log[i],
                   d_skip[i], ssm_norm[i], cmp_pe[i], cmp_w1[i], cmp_w2[i], ret_norm[i], w_branch[i], w_out[i],
                   ple_proj[i], ple_gate[i], ple_norm[i])
    return xf.reshape(b, L, d)
```

```python
import functools

import jax
import jax.numpy as jnp
import numpy as np
from jax import lax
from jax.experimental import pallas as pl
from jax.experimental.pallas import tpu as pltpu

f32 = jnp.float32
bf16 = jnp.bfloat16

D_MODEL = 2048
P_DIM = 256
EPS = 1e-6
NEG = -1e30
WIDTH = 2048

A_HEADS, A_HEAD_DIM, A_GROUPS, A_STATE, A_CONV, CHUNK = 32, 64, 8, 128, 4, 128
A_GW = (A_HEADS // A_GROUPS) * A_HEAD_DIM
A_CONV_CH = WIDTH + 2 * A_GROUPS * A_STATE
B_HEADS, B_KV, B_HD = 16, 4, 128
B_REP = B_HEADS // B_KV
CMP_LEN, CMP_STRIDE, CMP_HIDDEN = 32, 16, 256
SEL_LEN, SEL_TOPK, SEL_LOCAL, WINDOW, QB = 64, 16, 2, 512, 128
SEL_SHIFT = SEL_LEN.bit_length() - 1
assert SEL_LOCAL * SEL_LEN >= QB and QB % SEL_LEN == 0
LOG2E = 1.4426950408889634
KT = 1024
SEL_UNROLL = 2
C_HEADS, C_HD = 8, 256
ROPE_BASE = 10000.0

IN_SIZES = (2048, 2048, 1024, 1024, 32,
            2048, 512, 512, 512, 512, 512, 512, 48, 2048,
            2048, 2048, 2048, 2048, 6144)

C_XBC, C_ZA, C_QC, C_KC, C_ZC, C_ZB, C_GM = 0, 4096, 6144, 8192, 10240, 12288, 14336
C_KCB, C_VCB, C_GATES, C_DT, N32 = 20480, 20992, 21504, 22016, 22528
H_QB, H_VC, H_KS, H_VS, H_KW, H_VW, N16 = 0, 2048, 4096, 4608, 5120, 5632, 6144

LANES = 128
VMEM_LIMIT = 56 * 1024 * 1024


def _cparams(*sem):
    return pltpu.CompilerParams(dimension_semantics=sem, vmem_limit_bytes=VMEM_LIMIT)


def _sigmoid(x):
    return 1.0 / (1.0 + jnp.exp(-x))


def _silu(x):
    return x * _sigmoid(x)


def _dot(a, b):
    return jnp.dot(a, b, preferred_element_type=f32)


def _dot_nt(a, b):
    return lax.dot_general(a, b, (((1,), (1,)), ((), ())), preferred_element_type=f32)


def _dot_tn(a, b):
    return lax.dot_general(a, b, (((0,), (0,)), ((), ())), preferred_element_type=f32)


def _norm_matmul_body(x_ref, g_ref, w_ref, o_ref, h_ref):
    @pl.when(pl.program_id(1) == 0)
    def _():
        xf = x_ref[...]
        ms = jnp.mean(xf * xf, axis=-1, keepdims=True)
        h_ref[...] = (xf * lax.rsqrt(ms + EPS) * g_ref[...]).astype(bf16)

    o_ref[...] = _dot_nt(h_ref[...], w_ref[...]).astype(o_ref.dtype)


def norm_matmul(x, gain, w_t, out_dtype, name):
    m, k = x.shape
    n = w_t.shape[0]
    tm, tn = min(1024, m), 1024
    return pl.pallas_call(
        _norm_matmul_body,
        grid=(m // tm, n // tn),
        in_specs=[pl.BlockSpec((tm, k), lambda i, j: (i, 0)),
                  pl.BlockSpec((1, k), lambda i, j: (0, 0)),
                  pl.BlockSpec((tn, k), lambda i, j: (j, 0))],
        out_specs=pl.BlockSpec((tm, tn), lambda i, j: (i, j)),
        out_shape=jax.ShapeDtypeStruct((m, n), out_dtype),
        scratch_shapes=[pltpu.VMEM((tm, k), bf16)],
        compiler_params=_cparams("parallel", "arbitrary"),
        name=name,
    )(x, gain.reshape(1, k), w_t)


def _expand_heads(v, g, width):
    rows = v.shape[0]
    hd = width // 4
    lane = lax.broadcasted_iota(jnp.int32, (rows, width), 1)
    out = jnp.broadcast_to(v[:, 4 * g + 3:4 * g + 4], (rows, width))
    for r in (2, 1, 0):
        out = jnp.where(lane < (r + 1) * hd, jnp.broadcast_to(v[:, 4 * g + r:4 * g + r + 1], (rows, width)), out)
    return out


def _ssd_body(xbc_ref, z_ref, dt_ref, cw_ref, cb_ref, dtb_ref, alog_ref, dskip_ref, nw_ref,
              o_ref, tail_ref, st_ref, y_ref):
    c = pl.program_id(0)

    @pl.when(c == 0)
    def _():
        tail_ref[0:SUB, :] = jnp.zeros((SUB, A_CONV_CH), f32)
        st_ref[...] = jnp.zeros_like(st_ref)

    cur = xbc_ref[...]
    tail_ref[SUB:SUB + CHUNK, :] = cur
    acc = cur * cw_ref[A_CONV - 1:A_CONV, :] + cb_ref[...]
    for s in range(1, A_CONV):
        acc = acc + tail_ref[SUB - s:SUB - s + CHUNK, :] * cw_ref[A_CONV - 1 - s:A_CONV - s, :]
    tail_ref[0:SUB, :] = cur[CHUNK - SUB:CHUNK]
    xbc = _silu(acc)
    xs = xbc[:, 0:WIDTH]
    bm = xbc[:, WIDTH:WIDTH + A_GROUPS * A_STATE]
    cm = xbc[:, WIDTH + A_GROUPS * A_STATE:]

    xdt = dt_ref[...] + dtb_ref[...]
    dt = jnp.maximum(xdt, 0.0) + jnp.log1p(jnp.exp(-jnp.abs(xdt)))
    a = -jnp.exp(alog_ref[...]) * dt
    rowi = lax.broadcasted_iota(jnp.int32, (CHUNK, LANES), 0)
    a_cs = a
    s = 1
    while s < CHUNK:
        a_cs = a_cs + jnp.where(rowi >= s, pltpu.roll(a_cs, s, 0), 0.0)
        s *= 2
    a_cs_t = a_cs.T
    a_last = a_cs[CHUNK - 1:CHUNK, :]
    exp_acs = jnp.exp(a_cs)
    dend = jnp.exp(a_last - a_cs)
    cdecay = jnp.exp(a_last)

    li = lax.broadcasted_iota(jnp.int32, (CHUNK, CHUNK), 0)
    si = lax.broadcasted_iota(jnp.int32, (CHUNK, CHUNK), 1)
    causal = li >= si
    lane_g = lax.broadcasted_iota(jnp.int32, (CHUNK, A_GW), 1)

    for g in range(A_GROUPS):
        cm_g = cm[:, g * A_STATE:(g + 1) * A_STATE].astype(bf16)
        bm_g = bm[:, g * A_STATE:(g + 1) * A_STATE].astype(bf16)
        xs_g = xs[:, g * A_GW:(g + 1) * A_GW]
        cb = _dot_nt(cm_g, bm_g)
        xdt_g = xs_g * _expand_heads(dt, g, A_GW)
        xdt_b = xdt_g.astype(bf16)
        lhs, rhs = [], []
        for r in range(4):
            h = 4 * g + r
            seg = a_cs[:, h:h + 1] - a_cs_t[h:h + 1, :]
            dec = jnp.exp(jnp.where(causal, seg, -jnp.inf))
            lhs.append((cb * dec).astype(bf16))
            keep = (lane_g >= r * A_HEAD_DIM) & (lane_g < (r + 1) * A_HEAD_DIM)
            rhs.append(jnp.where(keep, xdt_b, jnp.zeros_like(xdt_b)))
        y_diag = _dot(jnp.concatenate(lhs, axis=1), jnp.concatenate(rhs, axis=0))
        st = st_ref[g]
        y_off = _dot(cm_g, st.astype(bf16)) * _expand_heads(exp_acs, g, A_GW)
        y_ref[:, g * A_GW:(g + 1) * A_GW] = y_diag + y_off + xs_g * dskip_ref[:, g * A_GW:(g + 1) * A_GW]
        wgt = (xdt_g * _expand_heads(dend, g, A_GW)).astype(bf16)
        st_ref[g] = st * _expand_heads(cdecay, g, A_GW) + _dot_tn(bm_g, wgt)

    yz = y_ref[...] * _silu(z_ref[...])
    ms = jnp.mean(yz * yz, axis=-1, keepdims=True)
    o_ref[...] = (yz * lax.rsqrt(ms + EPS) * nw_ref[...]).astype(o_ref.dtype)


def ssd_mixer(p32, conv_w, conv_b, dt_bias, a_log, d_skip, norm_w):
    L = p32.shape[0]
    pad = LANES - A_HEADS
    dtb = jnp.pad(dt_bias, (0, pad)).reshape(1, LANES)
    alog = jnp.pad(a_log, (0, pad)).reshape(1, LANES)
    dskip = jnp.repeat(d_skip, A_HEAD_DIM).reshape(1, WIDTH)
    const = lambda shape: pl.BlockSpec(shape, lambda c: (0,) * len(shape))
    return pl.pallas_call(
        _ssd_body,
        grid=(L // CHUNK,),
        in_specs=[pl.BlockSpec((CHUNK, A_CONV_CH), lambda c: (c, C_XBC // A_CONV_CH)),
                  pl.BlockSpec((CHUNK, WIDTH), lambda c: (c, C_ZA // WIDTH)),
                  pl.BlockSpec((CHUNK, LANES), lambda c: (c, C_DT // LANES)),
                  const((A_CONV, A_CONV_CH)), const((1, A_CONV_CH)), const((1, LANES)), const((1, LANES)),
                  const((1, WIDTH)), const((1, WIDTH))],
        out_specs=pl.BlockSpec((CHUNK, WIDTH), lambda c: (c, 0)),
        out_shape=jax.ShapeDtypeStruct((L, WIDTH), bf16),
        scratch_shapes=[pltpu.VMEM((SUB + CHUNK, A_CONV_CH), f32),
                        pltpu.VMEM((A_GROUPS, A_STATE, A_GW), f32),
                        pltpu.VMEM((CHUNK, WIDTH), f32)],
        compiler_params=_cparams("arbitrary"),
        name="ssd_mixer",
    )(p32, p32, p32, conv_w, conv_b.reshape(1, A_CONV_CH), dtb, alog, dskip, norm_w.reshape(1, WIDTH))


def _ret_body(q_ref, k_ref, z_ref, v_ref, cos_ref, sin_ref, dmat_ref, zeta_ref, xi_ref, cd_ref, gn_ref,
              o_ref, r_ref):
    c = pl.program_id(0)

    @pl.when(c == 0)
    def _():
        r_ref[...] = jnp.zeros_like(r_ref)

    cos = cos_ref[...]
    sin = sin_ref[...]
    half = C_HD // 2

    def rot(x):
        x1, x2 = x[:, :half], x[:, half:]
        return jnp.concatenate([x1 * cos - x2 * sin, x1 * sin + x2 * cos], axis=1)

    for h in range(C_HEADS):
        sl = slice(h * C_HD, (h + 1) * C_HD)
        qr = rot(q_ref[:, sl])
        kr = rot(k_ref[:, sl]) * (C_HD ** -0.5)
        qb, kb = qr.astype(bf16), kr.astype(bf16)
        v = v_ref[:, sl]
        s = _dot_nt(qb, kb) * dmat_ref[h]
        inner = _dot(s.astype(bf16), v)
        rst = r_ref[h]
        cross = _dot(qb, rst.astype(bf16)) * xi_ref[:, h:h + 1]
        o = inner + cross
        mu = jnp.mean(o, axis=-1, keepdims=True)
        d = o - mu
        var = jnp.mean(d * d, axis=-1, keepdims=True)
        on = d * lax.rsqrt(var + EPS) * gn_ref[:, sl]
        o_ref[:, sl] = (_silu(z_ref[:, sl]) * on).astype(o_ref.dtype)
        kz = (kr * zeta_ref[:, h:h + 1]).astype(bf16)
        r_ref[h] = rst * cd_ref[0:1, h:h + 1] + _dot_tn(kz, v)


def _retention_tables(L):
    T, H, d = CHUNK, C_HEADS, C_HD
    pos = jnp.arange(L)
    inv = ROPE_BASE ** (-jnp.arange(0, d, 2, dtype=f32) / d)
    ang = pos.astype(f32)[:, None] * inv[None, :]
    log_g = jnp.log1p(-jnp.exp2(-5.0 - jnp.arange(H, dtype=f32)))
    i = jnp.arange(T, dtype=f32)
    diff = i[:, None] - i[None, :]
    dmat = jnp.where(diff >= 0, jnp.exp(log_g[:, None, None] * jnp.maximum(diff, 0.0)), 0.0)
    zeta = jnp.exp(log_g[:, None] * (T - 1 - i)[None, :])
    xi = jnp.exp(log_g[:, None] * (i + 1.0)[None, :])
    cdecay = jnp.exp(log_g * T)
    padl = lambda t: jnp.pad(t, ((0, 0), (0, LANES - H)))
    return (jnp.cos(ang), jnp.sin(ang), dmat, padl(zeta.T), padl(xi.T), padl(cdecay[None, :]))


def retention_mixer(p32, p16, gn_w):
    L = p32.shape[0]
    cos, sin, dmat, zeta_t, xi_t, cdecay = _retention_tables(L)
    const = lambda shape: pl.BlockSpec(shape, lambda c: (0,) * len(shape))
    return pl.pallas_call(
        _ret_body,
        grid=(L // CHUNK,),
        in_specs=[pl.BlockSpec((CHUNK, WIDTH), lambda c: (c, C_QC // WIDTH)),
                  pl.BlockSpec((CHUNK, WIDTH), lambda c: (c, C_KC // WIDTH)),
                  pl.BlockSpec((CHUNK, WIDTH), lambda c: (c, C_ZC // WIDTH)),
                  pl.BlockSpec((CHUNK, WIDTH), lambda c: (c, H_VC // WIDTH)),
                  pl.BlockSpec((CHUNK, C_HD // 2), lambda c: (c, 0)),
                  pl.BlockSpec((CHUNK, C_HD // 2), lambda c: (c, 0)),
                  const((C_HEADS, CHUNK, CHUNK)), const((CHUNK, LANES)), const((CHUNK, LANES)),
                  const((1, LANES)), const((1, WIDTH))],
        out_specs=pl.BlockSpec((CHUNK, WIDTH), lambda c: (c, 0)),
        out_shape=jax.ShapeDtypeStruct((L, WIDTH), bf16),
        scratch_shapes=[pltpu.VMEM((C_HEADS, C_HD, C_HD), f32)],
        compiler_params=_cparams("arbitrary"),
        name="retention_mixer",
    )(p32, p32, p32, p16, cos, sin, dmat, zeta_t, xi_t, cdecay, gn_w.reshape(1, WIDTH))


def _compress_body(x_ref, pe_ref, w1_ref, w2_ref, o_ref):
    n = o_ref.shape[0]
    acc = [jnp.zeros((n, CMP_HIDDEN), f32) for _ in range(2)]
    for b in range(CMP_STRIDE):
        x = x_ref[pl.ds(b, n, stride=CMP_STRIDE), :]
        for a in range(2):
            acc[a] = acc[a] + _dot((x + pe_ref[a, b]).astype(bf16), w1_ref[a, b])
    pre = acc[0] + pltpu.roll(acc[1], n - 1, 0)
    out = _dot(_silu(pre).astype(bf16), w2_ref[...])
    row = lax.broadcasted_iota(jnp.int32, out.shape, 0)
    o_ref[...] = jnp.where(row < n - 1, out, 0.0).astype(o_ref.dtype)


def compress_kv(p32, cmp_pe, cmp_w1, cmp_w2):
    L = p32.shape[0]
    nblk = L // CMP_STRIDE
    w1 = cmp_w1.reshape(2, 2, CMP_STRIDE, B_HD, CMP_HIDDEN).astype(bf16)
    pe = cmp_pe.reshape(2, 2, CMP_STRIDE, 1, B_HD)
    w2 = cmp_w2.astype(bf16)
    return pl.pallas_call(
        _compress_body,
        grid=(2, B_KV),
        in_specs=[pl.BlockSpec((L, LANES), lambda w, h: (0, C_KCB // LANES + w * B_KV + h)),
                  pl.BlockSpec((None, 2, CMP_STRIDE, 1, B_HD), lambda w, h: (w, 0, 0, 0, 0)),
                  pl.BlockSpec((None, 2, CMP_STRIDE, B_HD, CMP_HIDDEN), lambda w, h: (w, 0, 0, 0, 0)),
                  pl.BlockSpec((None, CMP_HIDDEN, B_HD), lambda w, h: (w, 0, 0))],
        out_specs=pl.BlockSpec((None, None, nblk, B_HD), lambda w, h: (w, h, 0, 0)),
        out_shape=jax.ShapeDtypeStruct((2, B_KV, nblk, B_HD), bf16),
        compiler_params=_cparams("parallel", "parallel"),
        name="nsa_compress",
    )(p32, pe, w1, w2)


def _sort16_network():
    n, pairs, p = 16, [], 1
    while p < n:
        k = p
        while k >= 1:
            for j in range(k % p, n - k, 2 * k):
                for i in range(min(k, n - j - k)):
                    if (i + j) // (2 * p) == (i + j + k) // (2 * p):
                        pairs.append((i + j, i + j + k))
            k //= 2
        p *= 2
    return pairs


SUB = 8
NV = SEL_TOPK
NBLK = SUB * NV
assert NV == 16 and NBLK == LANES


def _topk_blocks_t(vt):
    v = [vt[SUB * a:SUB * (a + 1)] for a in range(NV)]
    s = list(v)
    for a, b in _sort16_network():
        s[a], s[b] = jnp.maximum(s[a], s[b]), jnp.minimum(s[a], s[b])
    for shift in (4, 2):
        s = [jnp.maximum(s[k], pltpu.roll(s[NV - 1 - k], shift, 0)) for k in range(NV)]
        d = NV // 2
        while d >= 1:
            for k in range(NV):
                if not k & d:
                    s[k], s[k + d] = jnp.maximum(s[k], s[k + d]), jnp.minimum(s[k], s[k + d])
            d //= 2
    top = [jnp.maximum(s[k], pltpu.roll(s[NV - 1 - k], 1, 0)) for k in range(NV)]
    thr = top[0]
    for k in range(1, NV):
        thr = jnp.minimum(thr, top[k])
    sub = lax.broadcasted_iota(jnp.int32, v[0].shape, 0)
    gt = [jnp.where(x > thr, 1.0, 0.0) for x in v]
    eq = [jnp.where(x == thr, 1.0, 0.0) for x in v]

    def sub_scan(x):
        for sh in (1, 2, 4):
            x = x + jnp.where(sub >= sh, pltpu.roll(x, sh, 0), 0.0)
        return x

    n_gt = gt[0]
    for a in range(1, NV):
        n_gt = n_gt + gt[a]
    n_gt = sub_scan(n_gt)
    need = float(SEL_TOPK) - jnp.broadcast_to(n_gt[SUB - 1:SUB], n_gt.shape)
    out, before = [], jnp.zeros_like(thr)
    for a in range(NV):
        inc = sub_scan(eq[a])
        out.append((gt[a] > 0.5) | ((eq[a] > 0.5) & (before + inc - eq[a] < need)))
        before = before + jnp.broadcast_to(inc[SUB - 1:SUB], inc.shape)
    return jnp.concatenate(out, axis=0)


def _nsa_body(q_ref, kc_ref, vc_ref, ks_ref, vs_ref, kw_ref, vw_ref, gate_ref, z_ref, o_ref,
              kaug_ref, vaug_ref, ov_ref):
    i = pl.program_id(1)
    L = ks_ref.shape[0]
    ncmp = kc_ref.shape[0]
    t0 = i * QB
    rows = B_REP * QB

    @pl.when(i == 0)
    def _():
        kaug_ref[:, 0:B_HD] = ks_ref[...]
        vaug_ref[:, 0:B_HD] = vs_ref[...]
        vaug_ref[:, B_HD:] = jnp.ones((L, B_HD), bf16)
        for c in range(L // KT):
            u = c * KT + lax.broadcasted_iota(jnp.int32, (KT, NBLK), 0)
            j = lax.broadcasted_iota(jnp.int32, (KT, NBLK), 1)
            kaug_ref[c * KT:(c + 1) * KT, B_HD:] = jnp.where(j == (u >> SEL_SHIFT), 1.0, 0.0).astype(bf16)
        ci = lax.broadcasted_iota(jnp.int32, (ncmp, NBLK), 0)
        sj = lax.broadcasted_iota(jnp.int32, (ncmp, NBLK), 1)
        ov_ref[...] = jnp.where((ci * CMP_STRIDE < (sj + 1) * SEL_LEN) & (ci * CMP_STRIDE + CMP_LEN > sj * SEL_LEN),
                                1.0, 0.0).astype(bf16)

    q = jnp.concatenate([q_ref[:, r * B_HD:(r + 1) * B_HD] for r in range(B_REP)], axis=0)
    q = (q.astype(f32) * (B_HD ** -0.5 * LOG2E)).astype(bf16)
    t_rows = t0 + (lax.broadcasted_iota(jnp.int32, (rows, 1), 0) & (QB - 1))
    t_tok = t0 + lax.broadcasted_iota(jnp.int32, (QB, 1), 0)

    def heads(x):
        return jnp.concatenate([x] * B_REP, axis=0)

    n_idx = lax.broadcasted_iota(jnp.int32, (QB, ncmp), 1)
    s_c = _dot_nt(q, kc_ref[...]) + heads(jnp.where(n_idx * CMP_STRIDE + (CMP_LEN - 1) <= t_tok, 0.0, NEG))
    e_c = jnp.exp2(s_c - jnp.max(s_c, axis=-1, keepdims=True))
    inv_c = jnp.where(t_rows >= CMP_LEN - 1, 1.0, 0.0) / jnp.sum(e_c, axis=-1, keepdims=True)
    p_c = e_c * inv_c
    o_c = _dot(p_c.astype(bf16), vc_ref[...])

    psum = p_c[0:QB]
    for r in range(1, B_REP):
        psum = psum + p_c[r * QB:(r + 1) * QB]
    overlap = ov_ref[...]
    imp = jnp.zeros((QB, NBLK), f32)
    rem = psum
    for _ in range(3):
        part = rem.astype(bf16)
        imp = imp + _dot(part, overlap)
        rem = rem - part.astype(f32)
    tq = t0 + lax.broadcasted_iota(jnp.int32, (QB, NBLK), 0)
    blk = lax.broadcasted_iota(jnp.int32, (QB, NBLK), 1)
    dist = (tq >> SEL_SHIFT) - blk
    forced = (blk == 0) | ((dist >= 0) & (dist < SEL_LOCAL))
    val = jnp.where(forced, jnp.inf, jnp.where(blk * SEL_LEN <= tq, imp, -jnp.inf))
    vt = val.T
    picked_t = _topk_blocks_t(vt) & (vt > -jnp.inf)

    td = pl.multiple_of(t0, QB)
    lane_d = lax.broadcasted_iota(jnp.int32, (QB, QB), 1)
    s_d = _dot_nt(q, ks_ref[pl.ds(td, QB), :]) + heads(jnp.where(t0 + lane_d <= t_tok, 0.0, NEG))
    m_d = jnp.max(s_d, axis=-1, keepdims=True)
    acc_d = _dot(jnp.exp2(s_d - m_d).astype(bf16), vaug_ref[pl.ds(td, QB), :])
    bsub = lax.broadcasted_iota(jnp.int32, (NBLK, QB), 0)
    bias = jnp.where(picked_t & (bsub * SEL_LEN < t0), 0.0, NEG).T.astype(bf16)
    qa = jnp.concatenate([q, jnp.concatenate([bias] * B_REP, axis=0)], axis=1)

    def sel_step(it, carry):
        m, acc = carry
        tiles = []
        for u in range(SEL_UNROLL):
            k0 = pl.multiple_of((it * SEL_UNROLL + u) * KT, KT)
            tiles.append((_dot_nt(qa, kaug_ref[pl.ds(k0, KT), :]), k0))
        for s, k0 in tiles:
            m_new = jnp.maximum(m, jnp.max(s, axis=-1, keepdims=True))
            p = jnp.exp2(s - m_new).astype(bf16)
            acc = jnp.exp2(m - m_new) * acc + _dot(p, vaug_ref[pl.ds(k0, KT), :])
            m = m_new
        return m, acc

    n_steps = (t0 + SEL_UNROLL * KT - 1) // (SEL_UNROLL * KT)
    _, acc_s = lax.fori_loop(0, n_steps, sel_step, (m_d, acc_d))
    o_s = acc_s[:, 0:B_HD] / acc_s[:, B_HD:]

    wlen = WINDOW + QB
    w0 = pl.multiple_of(jnp.maximum(t0 - WINDOW, 0), QB)
    kpos = w0 + lax.broadcasted_iota(jnp.int32, (QB, wlen), 1)
    s_w = _dot_nt(q, kw_ref[pl.ds(w0, wlen), :]) + heads(
        jnp.where((kpos <= t_tok) & (kpos > t_tok - WINDOW), 0.0, NEG))
    e_w = jnp.exp2(s_w - jnp.max(s_w, axis=-1, keepdims=True))
    o_w = _dot(e_w.astype(bf16), vw_ref[pl.ds(w0, wlen), :]) * (1.0 / jnp.sum(e_w, axis=-1, keepdims=True))

    ng = 3 * B_REP
    gt = _sigmoid(pltpu.roll(gate_ref[...], (LANES - ng * pl.program_id(0)) % LANES, 1))
    for r in range(B_REP):
        rs = slice(r * QB, (r + 1) * QB)
        o = (gt[:, 3 * r:3 * r + 1] * o_c[rs] + gt[:, 3 * r + 1:3 * r + 2] * o_s[rs]
             + gt[:, 3 * r + 2:3 * r + 3] * o_w[rs])
        o_ref[:, r * B_HD:(r + 1) * B_HD] = (o * _silu(z_ref[:, r * B_HD:(r + 1) * B_HD])).astype(o_ref.dtype)


def nsa_mixer(p32, p16, cmp):
    L = p32.shape[0]
    assert L // SEL_LEN <= NBLK and L % (SEL_UNROLL * KT) == 0
    ncmp = cmp.shape[2]
    gw = B_REP * B_HD
    kv = lambda off: pl.BlockSpec((L, B_HD), lambda g, i: (0, off // B_HD + g))
    return pl.pallas_call(
        _nsa_body,
        grid=(B_KV, L // QB),
        in_specs=[pl.BlockSpec((QB, gw), lambda g, i: (i, H_QB // gw + g)),
                  pl.BlockSpec((None, None, ncmp, B_HD), lambda g, i: (0, g, 0, 0)),
                  pl.BlockSpec((None, None, ncmp, B_HD), lambda g, i: (1, g, 0, 0)),
                  kv(H_KS), kv(H_VS), kv(H_KW), kv(H_VW),
                  pl.BlockSpec((QB, LANES), lambda g, i: (i, C_GATES // LANES)),
                  pl.BlockSpec((QB, gw), lambda g, i: (i, C_ZB // gw + g))],
        out_specs=pl.BlockSpec((QB, gw), lambda g, i: (i, g)),
        out_shape=jax.ShapeDtypeStruct((L, WIDTH), bf16),
        scratch_shapes=[pltpu.VMEM((L, B_HD + NBLK), bf16), pltpu.VMEM((L, 2 * B_HD), bf16),
                        pltpu.VMEM((ncmp, NBLK), bf16)],
        compiler_params=_cparams("parallel", "arbitrary"),
        name="nsa_attention",
    )(p16, cmp, cmp, p16, p16, p16, p16, p32, p32)


def _merge_body(ya_ref, yb_ref, yc_ref, w_ref, ga_ref, gb_ref, gc_ref, o_ref):
    acc = _sigmoid(ga_ref[...]) * _dot(ya_ref[...], w_ref[0])
    acc = acc + _sigmoid(gb_ref[...]) * _dot(yb_ref[...], w_ref[1])
    acc = acc + _sigmoid(gc_ref[...]) * _dot(yc_ref[...], w_ref[2])
    o_ref[...] = acc.astype(o_ref.dtype)


def merge_branches(ya, yb, yc, w_branch, p32):
    L = ya.shape[0]
    tm, tn = min(512, L), 512
    yspec = pl.BlockSpec((tm, WIDTH), lambda j, i: (i, 0))
    gspec = lambda b: pl.BlockSpec((tm, tn), lambda j, i: (i, (C_GM + b * D_MODEL) // tn + j))
    return pl.pallas_call(
        _merge_body,
        grid=(D_MODEL // tn, L // tm),
        in_specs=[yspec, yspec, yspec,
                  pl.BlockSpec((3, WIDTH, tn), lambda j, i: (0, 0, j)),
                  gspec(0), gspec(1), gspec(2)],
        out_specs=pl.BlockSpec((tm, tn), lambda j, i: (i, j)),
        out_shape=jax.ShapeDtypeStruct((L, D_MODEL), bf16),
        compiler_params=_cparams("parallel", "parallel"),
        name="branch_merge",
    )(ya, yb, yc, w_branch.astype(bf16), p32, p32, p32)


def _rms(x):
    return x * lax.rsqrt(jnp.mean(x * x, axis=-1, keepdims=True) + EPS)


def _post_body(x_ref, m_ref, p_ref, wo_ref, wp_ref, wg_ref, npost_ref, nple_ref, o_ref):
    x1 = x_ref[...] + _rms(_dot(m_ref[...], wo_ref[...])) * npost_ref[...]
    e = _dot(p_ref[...].astype(bf16), wp_ref[...])
    g = _sigmoid(_dot(_rms(x1).astype(bf16), wg_ref[...]))
    o_ref[...] = x1 + _rms(g * e) * nple_ref[...]


def post_mixer(x, merged, p, w_out, ple_proj, ple_gate, norm_post, ple_norm):
    L = x.shape[0]
    tm = min(256, L)
    const = lambda shape: pl.BlockSpec(shape, lambda i: (0,) * len(shape))
    return pl.pallas_call(
        _post_body,
        grid=(L // tm,),
        in_specs=[pl.BlockSpec((tm, D_MODEL), lambda i: (i, 0)),
                  pl.BlockSpec((tm, D_MODEL), lambda i: (i, 0)),
                  pl.BlockSpec((tm, P_DIM), lambda i: (i, 0)),
                  const((D_MODEL, D_MODEL)), const((P_DIM, D_MODEL)), const((D_MODEL, D_MODEL)),
                  const((1, D_MODEL)), const((1, D_MODEL))],
        out_specs=pl.BlockSpec((tm, D_MODEL), lambda i: (i, 0)),
        out_shape=jax.ShapeDtypeStruct((L, D_MODEL), f32),
        compiler_params=_cparams("parallel"),
        name="post_mixer",
    )(x, merged, p, w_out.astype(bf16), ple_proj.astype(bf16), ple_gate.astype(bf16),
      norm_post.reshape(1, D_MODEL), ple_norm.reshape(1, D_MODEL))


PACK_T = 512


def _pack_sources():
    names = ("xa", "za", "ba", "ca", "dta", "qb", "kcb", "vcb", "ksb", "vsb", "kwb", "vwb", "gb", "zb",
             "qc", "kc", "vc", "zc", "gm")
    off = dict(zip(names, [0] + np.cumsum(IN_SIZES)[:-1].tolist()))
    size = dict(zip(names, IN_SIZES))
    tiles = lambda *sec: [(off[s] + PACK_T * t, int(s in ("qc", "kc"))) for s in sec for t in range(size[s] // PACK_T)]
    src32 = tiles("xa", "ba", "ca", "za", "qc", "kc", "zc", "zb", "gm", "kcb", "vcb") + [(off["gb"], 0), (off["dta"], 0)]
    src16 = tiles("qb", "vc", "ksb", "vsb", "kwb", "vwb")
    assert len(src32) * PACK_T == N32 and len(src16) * PACK_T == N16
    return src32, src16


def _pack_body(src_ref, split_ref, w_ref, o_ref):
    j = pl.program_id(0)

    @pl.when(split_ref[j] == 0)
    def _():
        o_ref[...] = w_ref[0].astype(o_ref.dtype)

    @pl.when(split_ref[j] != 0)
    def _():
        half = C_HD // 2
        r = lax.broadcasted_iota(jnp.int32, (PACK_T, PACK_T), 0)
        c = lax.broadcasted_iota(jnp.int32, (PACK_T, PACK_T), 1)
        hs = half.bit_length() - 1
        src_row = ((r >> (hs + 1)) << (hs + 1)) + 2 * (r & (half - 1)) + ((r >> hs) & 1)
        perm = jnp.where(c == src_row, 1.0, 0.0).astype(bf16)
        o_ref[...] = _dot(perm, w_ref[0].astype(bf16)).astype(o_ref.dtype)


def pack_w_in(w_in_t, layer_idx, src):
    k = w_in_t.shape[2]
    assert all(s % SUB == 0 for s, _ in src)
    grid_spec = pltpu.PrefetchScalarGridSpec(
        num_scalar_prefetch=2, grid=(len(src),),
        in_specs=[pl.BlockSpec((pl.Element(1), pl.Element(PACK_T), pl.Element(k)),
                               lambda j, src8, split: (layer_idx, src8[j] * SUB, 0))],
        out_specs=pl.BlockSpec((PACK_T, k), lambda j, src8, split: (j, 0)))
    return pl.pallas_call(
        _pack_body, grid_spec=grid_spec,
        out_shape=jax.ShapeDtypeStruct((len(src) * PACK_T, k), bf16),
        compiler_params=_cparams("parallel"),
        name="pack_w_in",
    )(jnp.asarray([s // SUB for s, _ in src], jnp.int32), jnp.asarray([f for _, f in src], jnp.int32), w_in_t)


def layer(i, x, p, norm_pre, norm_post, w_in_t, conv_w, conv_b, dt_bias, a_log, d_skip, ssm_norm,
          cmp_pe, cmp_w1, cmp_w2, ret_norm, w_branch, w_out, ple_proj, ple_gate, ple_norm):
    src32, src16 = _pack_sources()
    w32, w16 = pack_w_in(w_in_t, i, src32), pack_w_in(w_in_t, i, src16)
    p32 = norm_matmul(x, norm_pre, w32, f32, "in_proj_f32")
    p16 = norm_matmul(x, norm_pre, w16, bf16, "in_proj_bf16")
    ya = ssd_mixer(p32, conv_w, conv_b, dt_bias, a_log, d_skip, ssm_norm)
    cmp = compress_kv(p32, cmp_pe, cmp_w1, cmp_w2)
    yb = nsa_mixer(p32, p16, cmp)
    yc = retention_mixer(p32, p16, ret_norm)
    merged = merge_branches(ya, yb, yc, w_branch, p32)
    return post_mixer(x, merged, p, w_out, ple_proj, ple_gate, norm_post, ple_norm)


def kernel(x, p, norm_pre, norm_post, w_in, conv_w, conv_b, dt_bias, a_log, d_skip, ssm_norm, cmp_pe, cmp_w1,
           cmp_w2, ret_norm, w_branch, w_out, ple_proj, ple_gate, ple_norm):
    b, L, d = x.shape
    assert b == 1 and d == D_MODEL
    xf = x.reshape(L, d)
    w_in_t = jnp.swapaxes(w_in, 1, 2)
    for i in range(p.shape[0]):
        xf = layer(i, xf, p[i, 0], norm_pre[i], norm_post[i], w_in_t, conv_w[i], conv_b[i], dt_bias[i], a_log[i],
                   d_skip[i], ssm_norm[i], cmp_pe[i], cmp_w1[i], cmp_w2[i], ret_norm[i], w_branch[i], w_out[i],
                   ple_proj[i], ple_gate[i], ple_norm[i])
    return xf.reshape(b, L, d)
```

```python
import functools

import jax
import jax.numpy as jnp
import numpy as np
from jax import lax
from jax.experimental import pallas as pl
from jax.experimental.pallas import tpu as pltpu

f32 = jnp.float32
bf16 = jnp.bfloat16

D_MODEL = 2048
P_DIM = 256
EPS = 1e-6
NEG = -1e30
WIDTH = 2048

A_HEADS, A_HEAD_DIM, A_GROUPS, A_STATE, A_CONV, CHUNK = 32, 64, 8, 128, 4, 128
A_GW = (A_HEADS // A_GROUPS) * A_HEAD_DIM
A_CONV_CH = WIDTH + 2 * A_GROUPS * A_STATE
B_HEADS, B_KV, B_HD = 16, 4, 128
B_REP = B_HEADS // B_KV
CMP_LEN, CMP_STRIDE, CMP_HIDDEN = 32, 16, 256
SEL_LEN, SEL_TOPK, SEL_LOCAL, WINDOW, QB = 64, 16, 2, 512, 128
SEL_SHIFT = SEL_LEN.bit_length() - 1
assert SEL_LOCAL * SEL_LEN >= QB and QB % SEL_LEN == 0
LOG2E = 1.4426950408889634
KT = 1024
SEL_UNROLL = 2
C_HEADS, C_HD = 8, 256
ROPE_BASE = 10000.0

IN_SIZES = (2048, 2048, 1024, 1024, 32,
            2048, 512, 512, 512, 512, 512, 512, 48, 2048,
            2048, 2048, 2048, 2048, 6144)

C_XBC, C_ZA, C_QC, C_KC, C_ZC, C_ZB, C_GM = 0, 4096, 6144, 8192, 10240, 12288, 14336
C_KCB, C_VCB, C_GATES, C_DT, N32 = 20480, 20992, 21504, 22016, 22528
H_QB, H_VC, H_KS, H_VS, H_KW, H_VW, N16 = 0, 2048, 4096, 4608, 5120, 5632, 6144

LANES = 128
VMEM_LIMIT = 56 * 1024 * 1024


def _cparams(*sem):
    return pltpu.CompilerParams(dimension_semantics=sem, vmem_limit_bytes=VMEM_LIMIT)


def _sigmoid(x):
    return 1.0 / (1.0 + jnp.exp(-x))


def _silu(x):
    return x * _sigmoid(x)


def _dot(a, b):
    return jnp.dot(a, b, preferred_element_type=f32)


def _dot_nt(a, b):
    return lax.dot_general(a, b, (((1,), (1,)), ((), ())), preferred_element_type=f32)


def _dot_tn(a, b):
    return lax.dot_general(a, b, (((0,), (0,)), ((), ())), preferred_element_type=f32)


def _pre_norm_body(x_ref, g_ref, o_ref):
    xf = x_ref[...]
    ms = jnp.mean(xf * xf, axis=-1, keepdims=True)
    o_ref[...] = (xf * lax.rsqrt(ms + EPS) * g_ref[...]).astype(o_ref.dtype)


def pre_norm(x, gain):
    m, k = x.shape
    tm = min(512, m)
    return pl.pallas_call(
        _pre_norm_body,
        grid=(m // tm,),
        in_specs=[pl.BlockSpec((tm, k), lambda i: (i, 0)), pl.BlockSpec((1, k), lambda i: (0, 0))],
        out_specs=pl.BlockSpec((tm, k), lambda i: (i, 0)),
        out_shape=jax.ShapeDtypeStruct((m, k), bf16),
        compiler_params=_cparams("parallel"),
        name="pre_norm",
    )(x, gain.reshape(1, k))


PACK_T = 512


def _in_proj_body(src_ref, split_ref, h_ref, w_ref, o_ref, wb_ref):
    j = pl.program_id(0)
    first = pl.program_id(1) == 0

    @pl.when(first & (split_ref[j] == 0))
    def _():
        wb_ref[...] = w_ref[0].astype(bf16)

    @pl.when(first & (split_ref[j] != 0))
    def _():
        half = C_HD // 2
        hs = half.bit_length() - 1
        r = lax.broadcasted_iota(jnp.int32, (PACK_T, PACK_T), 0)
        c = lax.broadcasted_iota(jnp.int32, (PACK_T, PACK_T), 1)
        src_row = ((r >> (hs + 1)) << (hs + 1)) + 2 * (r & (half - 1)) + ((r >> hs) & 1)
        perm = jnp.where(c == src_row, 1.0, 0.0).astype(bf16)
        wb_ref[...] = _dot(perm, w_ref[0].astype(bf16)).astype(bf16)

    o_ref[...] = _dot_nt(h_ref[...], wb_ref[...]).astype(o_ref.dtype)


def in_proj(h, w_in_t, layer_idx, src, out_dtype, name):
    m, k = h.shape
    tm = min(2048, m)
    assert all(s % SUB == 0 for s, _ in src)
    grid_spec = pltpu.PrefetchScalarGridSpec(
        num_scalar_prefetch=2, grid=(len(src), m // tm),
        in_specs=[pl.BlockSpec((tm, k), lambda j, i, src8, split: (i, 0)),
                  pl.BlockSpec((pl.Element(1), pl.Element(PACK_T), pl.Element(k)),
                               lambda j, i, src8, split: (layer_idx, src8[j] * SUB, 0))],
        out_specs=pl.BlockSpec((tm, PACK_T), lambda j, i, src8, split: (i, j)),
        scratch_shapes=[pltpu.VMEM((PACK_T, k), bf16)])
    return pl.pallas_call(
        _in_proj_body, grid_spec=grid_spec,
        out_shape=jax.ShapeDtypeStruct((m, len(src) * PACK_T), out_dtype),
        compiler_params=_cparams("parallel", "arbitrary"),
        name=name,
    )(jnp.asarray([s // SUB for s, _ in src], jnp.int32), jnp.asarray([f for _, f in src], jnp.int32), h, w_in_t)


def _expand_heads(v, g, width):
    rows = v.shape[0]
    hd = width // 4
    lane = lax.broadcasted_iota(jnp.int32, (rows, width), 1)
    out = jnp.broadcast_to(v[:, 4 * g + 3:4 * g + 4], (rows, width))
    for r in (2, 1, 0):
        out = jnp.where(lane < (r + 1) * hd, jnp.broadcast_to(v[:, 4 * g + r:4 * g + r + 1], (rows, width)), out)
    return out


def _ssd_body(xbc_ref, z_ref, dt_ref, cw_ref, cb_ref, dtb_ref, alog_ref, dskip_ref, nw_ref,
              o_ref, tail_ref, st_ref, y_ref):
    c = pl.program_id(0)

    @pl.when(c == 0)
    def _():
        tail_ref[0:SUB, :] = jnp.zeros((SUB, A_CONV_CH), f32)
        st_ref[...] = jnp.zeros_like(st_ref)

    cur = xbc_ref[...]
    tail_ref[SUB:SUB + CHUNK, :] = cur
    acc = cur * cw_ref[A_CONV - 1:A_CONV, :] + cb_ref[...]
    for s in range(1, A_CONV):
        acc = acc + tail_ref[SUB - s:SUB - s + CHUNK, :] * cw_ref[A_CONV - 1 - s:A_CONV - s, :]
    tail_ref[0:SUB, :] = cur[CHUNK - SUB:CHUNK]
    xbc = _silu(acc)
    xs = xbc[:, 0:WIDTH]
    bm = xbc[:, WIDTH:WIDTH + A_GROUPS * A_STATE]
    cm = xbc[:, WIDTH + A_GROUPS * A_STATE:]

    xdt = dt_ref[...] + dtb_ref[...]
    dt = jnp.maximum(xdt, 0.0) + jnp.log1p(jnp.exp(-jnp.abs(xdt)))
    a = -jnp.exp(alog_ref[...]) * dt
    rowi = lax.broadcasted_iota(jnp.int32, (CHUNK, LANES), 0)
    a_cs = a
    s = 1
    while s < CHUNK:
        a_cs = a_cs + jnp.where(rowi >= s, pltpu.roll(a_cs, s, 0), 0.0)
        s *= 2
    a_cs_t = a_cs.T
    a_last = a_cs[CHUNK - 1:CHUNK, :]
    exp_acs = jnp.exp(a_cs)
    dend = jnp.exp(a_last - a_cs)
    cdecay = jnp.exp(a_last)

    li = lax.broadcasted_iota(jnp.int32, (CHUNK, CHUNK), 0)
    si = lax.broadcasted_iota(jnp.int32, (CHUNK, CHUNK), 1)
    causal = li >= si
    lane_g = lax.broadcasted_iota(jnp.int32, (CHUNK, A_GW), 1)

    for g in range(A_GROUPS):
        cm_g = cm[:, g * A_STATE:(g + 1) * A_STATE].astype(bf16)
        bm_g = bm[:, g * A_STATE:(g + 1) * A_STATE].astype(bf16)
        xs_g = xs[:, g * A_GW:(g + 1) * A_GW]
        cb = _dot_nt(cm_g, bm_g)
        xdt_g = xs_g * _expand_heads(dt, g, A_GW)
        xdt_b = xdt_g.astype(bf16)
        lhs, rhs = [], []
        for r in range(4):
            h = 4 * g + r
            seg = a_cs[:, h:h + 1] - a_cs_t[h:h + 1, :]
            dec = jnp.exp(jnp.where(causal, seg, -jnp.inf))
            lhs.append((cb * dec).astype(bf16))
            keep = (lane_g >= r * A_HEAD_DIM) & (lane_g < (r + 1) * A_HEAD_DIM)
            rhs.append(jnp.where(keep, xdt_b, jnp.zeros_like(xdt_b)))
        y_diag = _dot(jnp.concatenate(lhs, axis=1), jnp.concatenate(rhs, axis=0))
        st = st_ref[g]
        y_off = _dot(cm_g, st.astype(bf16)) * _expand_heads(exp_acs, g, A_GW)
        y_ref[:, g * A_GW:(g + 1) * A_GW] = y_diag + y_off + xs_g * dskip_ref[:, g * A_GW:(g + 1) * A_GW]
        wgt = (xdt_g * _expand_heads(dend, g, A_GW)).astype(bf16)
        st_ref[g] = st * _expand_heads(cdecay, g, A_GW) + _dot_tn(bm_g, wgt)

    yz = y_ref[...] * _silu(z_ref[...])
    ms = jnp.mean(yz * yz, axis=-1, keepdims=True)
    o_ref[...] = (yz * lax.rsqrt(ms + EPS) * nw_ref[...]).astype(o_ref.dtype)


def ssd_mixer(p32, conv_w, conv_b, dt_bias, a_log, d_skip, norm_w):
    L = p32.shape[0]
    pad = LANES - A_HEADS
    dtb = jnp.pad(dt_bias, (0, pad)).reshape(1, LANES)
    alog = jnp.pad(a_log, (0, pad)).reshape(1, LANES)
    dskip = jnp.repeat(d_skip, A_HEAD_DIM).reshape(1, WIDTH)
    const = lambda shape: pl.BlockSpec(shape, lambda c: (0,) * len(shape))
    return pl.pallas_call(
        _ssd_body,
        grid=(L // CHUNK,),
        in_specs=[pl.BlockSpec((CHUNK, A_CONV_CH), lambda c: (c, C_XBC // A_CONV_CH)),
                  pl.BlockSpec((CHUNK, WIDTH), lambda c: (c, C_ZA // WIDTH)),
                  pl.BlockSpec((CHUNK, LANES), lambda c: (c, C_DT // LANES)),
                  const((A_CONV, A_CONV_CH)), const((1, A_CONV_CH)), const((1, LANES)), const((1, LANES)),
                  const((1, WIDTH)), const((1, WIDTH))],
        out_specs=pl.BlockSpec((CHUNK, WIDTH), lambda c: (c, 0)),
        out_shape=jax.ShapeDtypeStruct((L, WIDTH), bf16),
        scratch_shapes=[pltpu.VMEM((SUB + CHUNK, A_CONV_CH), f32),
                        pltpu.VMEM((A_GROUPS, A_STATE, A_GW), f32),
                        pltpu.VMEM((CHUNK, WIDTH), f32)],
        compiler_params=_cparams("arbitrary"),
        name="ssd_mixer",
    )(p32, p32, p32, conv_w, conv_b.reshape(1, A_CONV_CH), dtb, alog, dskip, norm_w.reshape(1, WIDTH))


def _ret_body(q_ref, k_ref, z_ref, v_ref, cos_ref, sin_ref, dmat_ref, zeta_ref, xi_ref, cd_ref, gn_ref,
              o_ref, r_ref):
    c = pl.program_id(0)

    @pl.when(c == 0)
    def _():
        r_ref[...] = jnp.zeros_like(r_ref)

    cos = cos_ref[...]
    sin = sin_ref[...]
    half = C_HD // 2

    def rot(x):
        x1, x2 = x[:, :half], x[:, half:]
        return jnp.concatenate([x1 * cos - x2 * sin, x1 * sin + x2 * cos], axis=1)

    for h in range(C_HEADS):
        sl = slice(h * C_HD, (h + 1) * C_HD)
        qr = rot(q_ref[:, sl])
        kr = rot(k_ref[:, sl]) * (C_HD ** -0.5)
        qb, kb = qr.astype(bf16), kr.astype(bf16)
        v = v_ref[:, sl]
        s = _dot_nt(qb, kb) * dmat_ref[h]
        inner = _dot(s.astype(bf16), v)
        rst = r_ref[h]
        cross = _dot(qb, rst.astype(bf16)) * xi_ref[:, h:h + 1]
        o = inner + cross
        mu = jnp.mean(o, axis=-1, keepdims=True)
        d = o - mu
        var = jnp.mean(d * d, axis=-1, keepdims=True)
        on = d * lax.rsqrt(var + EPS) * gn_ref[:, sl]
        o_ref[:, sl] = (_silu(z_ref[:, sl]) * on).astype(o_ref.dtype)
        kz = (kr * zeta_ref[:, h:h + 1]).astype(bf16)
        r_ref[h] = rst * cd_ref[0:1, h:h + 1] + _dot_tn(kz, v)


def _retention_tables(L):
    T, H, d = CHUNK, C_HEADS, C_HD
    pos = jnp.arange(L)
    inv = ROPE_BASE ** (-jnp.arange(0, d, 2, dtype=f32) / d)
    ang = pos.astype(f32)[:, None] * inv[None, :]
    log_g = jnp.log1p(-jnp.exp2(-5.0 - jnp.arange(H, dtype=f32)))
    i = jnp.arange(T, dtype=f32)
    diff = i[:, None] - i[None, :]
    dmat = jnp.where(diff >= 0, jnp.exp(log_g[:, None, None] * jnp.maximum(diff, 0.0)), 0.0)
    zeta = jnp.exp(log_g[:, None] * (T - 1 - i)[None, :])
    xi = jnp.exp(log_g[:, None] * (i + 1.0)[None, :])
    cdecay = jnp.exp(log_g * T)
    padl = lambda t: jnp.pad(t, ((0, 0), (0, LANES - H)))
    return (jnp.cos(ang), jnp.sin(ang), dmat, padl(zeta.T), padl(xi.T), padl(cdecay[None, :]))


def retention_mixer(p32, p16, gn_w):
    L = p32.shape[0]
    cos, sin, dmat, zeta_t, xi_t, cdecay = _retention_tables(L)
    const = lambda shape: pl.BlockSpec(shape, lambda c: (0,) * len(shape))
    return pl.pallas_call(
        _ret_body,
        grid=(L // CHUNK,),
        in_specs=[pl.BlockSpec((CHUNK, WIDTH), lambda c: (c, C_QC // WIDTH)),
                  pl.BlockSpec((CHUNK, WIDTH), lambda c: (c, C_KC // WIDTH)),
                  pl.BlockSpec((CHUNK, WIDTH), lambda c: (c, C_ZC // WIDTH)),
                  pl.BlockSpec((CHUNK, WIDTH), lambda c: (c, H_VC // WIDTH)),
                  pl.BlockSpec((CHUNK, C_HD // 2), lambda c: (c, 0)),
                  pl.BlockSpec((CHUNK, C_HD // 2), lambda c: (c, 0)),
                  const((C_HEADS, CHUNK, CHUNK)), const((CHUNK, LANES)), const((CHUNK, LANES)),
                  const((1, LANES)), const((1, WIDTH))],
        out_specs=pl.BlockSpec((CHUNK, WIDTH), lambda c: (c, 0)),
        out_shape=jax.ShapeDtypeStruct((L, WIDTH), bf16),
        scratch_shapes=[pltpu.VMEM((C_HEADS, C_HD, C_HD), f32)],
        compiler_params=_cparams("arbitrary"),
        name="retention_mixer",
    )(p32, p32, p32, p16, cos, sin, dmat, zeta_t, xi_t, cdecay, gn_w.reshape(1, WIDTH))


def _compress_body(x_ref, pe_ref, w1_ref, w2_ref, o_ref):
    n = o_ref.shape[0]
    acc = [jnp.zeros((n, CMP_HIDDEN), f32) for _ in range(2)]
    for b in range(CMP_STRIDE):
        x = x_ref[pl.ds(b, n, stride=CMP_STRIDE), :]
        for a in range(2):
            acc[a] = acc[a] + _dot((x + pe_ref[a, b]).astype(bf16), w1_ref[a, b])
    pre = acc[0] + pltpu.roll(acc[1], n - 1, 0)
    out = _dot(_silu(pre).astype(bf16), w2_ref[...])
    row = lax.broadcasted_iota(jnp.int32, out.shape, 0)
    o_ref[...] = jnp.where(row < n - 1, out, 0.0).astype(o_ref.dtype)


def compress_kv(p32, cmp_pe, cmp_w1, cmp_w2):
    L = p32.shape[0]
    nblk = L // CMP_STRIDE
    w1 = cmp_w1.reshape(2, 2, CMP_STRIDE, B_HD, CMP_HIDDEN).astype(bf16)
    pe = cmp_pe.reshape(2, 2, CMP_STRIDE, 1, B_HD)
    w2 = cmp_w2.astype(bf16)
    return pl.pallas_call(
        _compress_body,
        grid=(2, B_KV),
        in_specs=[pl.BlockSpec((L, LANES), lambda w, h: (0, C_KCB // LANES + w * B_KV + h)),
                  pl.BlockSpec((None, 2, CMP_STRIDE, 1, B_HD), lambda w, h: (w, 0, 0, 0, 0)),
                  pl.BlockSpec((None, 2, CMP_STRIDE, B_HD, CMP_HIDDEN), lambda w, h: (w, 0, 0, 0, 0)),
                  pl.BlockSpec((None, CMP_HIDDEN, B_HD), lambda w, h: (w, 0, 0))],
        out_specs=pl.BlockSpec((None, None, nblk, B_HD), lambda w, h: (w, h, 0, 0)),
        out_shape=jax.ShapeDtypeStruct((2, B_KV, nblk, B_HD), bf16),
        compiler_params=_cparams("parallel", "parallel"),
        name="nsa_compress",
    )(p32, pe, w1, w2)


def _sort16_network():
    n, pairs, p = 16, [], 1
    while p < n:
        k = p
        while k >= 1:
            for j in range(k % p, n - k, 2 * k):
                for i in range(min(k, n - j - k)):
                    if (i + j) // (2 * p) == (i + j + k) // (2 * p):
                        pairs.append((i + j, i + j + k))
            k //= 2
        p *= 2
    return pairs


SUB = 8
NV = SEL_TOPK
NBLK = SUB * NV
assert NV == 16 and NBLK == LANES


def _topk_blocks_t(vt):
    v = [vt[SUB * a:SUB * (a + 1)] for a in range(NV)]
    s = list(v)
    for a, b in _sort16_network():
        s[a], s[b] = jnp.maximum(s[a], s[b]), jnp.minimum(s[a], s[b])
    for shift in (4, 2):
        s = [jnp.maximum(s[k], pltpu.roll(s[NV - 1 - k], shift, 0)) for k in range(NV)]
        d = NV // 2
        while d >= 1:
            for k in range(NV):
                if not k & d:
                    s[k], s[k + d] = jnp.maximum(s[k], s[k + d]), jnp.minimum(s[k], s[k + d])
            d //= 2
    top = [jnp.maximum(s[k], pltpu.roll(s[NV - 1 - k], 1, 0)) for k in range(NV)]
    thr = top[0]
    for k in range(1, NV):
        thr = jnp.minimum(thr, top[k])
    sub = lax.broadcasted_iota(jnp.int32, v[0].shape, 0)
    gt = [jnp.where(x > thr, 1.0, 0.0) for x in v]
    eq = [jnp.where(x == thr, 1.0, 0.0) for x in v]

    def sub_scan(x):
        for sh in (1, 2, 4):
            x = x + jnp.where(sub >= sh, pltpu.roll(x, sh, 0), 0.0)
        return x

    n_gt = gt[0]
    for a in range(1, NV):
        n_gt = n_gt + gt[a]
    n_gt = sub_scan(n_gt)
    need = float(SEL_TOPK) - jnp.broadcast_to(n_gt[SUB - 1:SUB], n_gt.shape)
    out, before = [], jnp.zeros_like(thr)
    for a in range(NV):
        inc = sub_scan(eq[a])
        out.append((gt[a] > 0.5) | ((eq[a] > 0.5) & (before + inc - eq[a] < need)))
        before = before + jnp.broadcast_to(inc[SUB - 1:SUB], inc.shape)
    return jnp.concatenate(out, axis=0)


def _nsa_body(q_ref, kc_ref, vc_ref, ks_ref, vs_ref, kw_ref, vw_ref, gate_ref, z_ref, o_ref,
              kaug_ref, vaug_ref, ov_ref):
    i = pl.program_id(1)
    L = ks_ref.shape[0]
    ncmp = kc_ref.shape[0]
    t0 = i * QB
    rows = B_REP * QB

    @pl.when(i == 0)
    def _():
        kaug_ref[:, 0:B_HD] = ks_ref[...]
        vaug_ref[:, 0:B_HD] = vs_ref[...]
        vaug_ref[:, B_HD:] = jnp.ones((L, B_HD), bf16)
        for c in range(L // KT):
            u = c * KT + lax.broadcasted_iota(jnp.int32, (KT, NBLK), 0)
            j = lax.broadcasted_iota(jnp.int32, (KT, NBLK), 1)
            kaug_ref[c * KT:(c + 1) * KT, B_HD:] = jnp.where(j == (u >> SEL_SHIFT), 1.0, 0.0).astype(bf16)
        ci = lax.broadcasted_iota(jnp.int32, (ncmp, NBLK), 0)
        sj = lax.broadcasted_iota(jnp.int32, (ncmp, NBLK), 1)
        ov_ref[...] = jnp.where((ci * CMP_STRIDE < (sj + 1) * SEL_LEN) & (ci * CMP_STRIDE + CMP_LEN > sj * SEL_LEN),
                                1.0, 0.0).astype(bf16)

    q = jnp.concatenate([q_ref[:, r * B_HD:(r + 1) * B_HD] for r in range(B_REP)], axis=0)
    q = (q.astype(f32) * (B_HD ** -0.5 * LOG2E)).astype(bf16)
    t_rows = t0 + (lax.broadcasted_iota(jnp.int32, (rows, 1), 0) & (QB - 1))

    def heads(x):
        return jnp.concatenate([x] * B_REP, axis=0)

    wlen = WINDOW + QB
    w0 = pl.multiple_of(jnp.maximum(t0 - WINDOW, 0), QB)
    gap_w = lax.broadcasted_iota(jnp.int32, (QB, wlen), 1) - lax.broadcasted_iota(jnp.int32, (QB, wlen), 0)
    s_w = _dot_nt(q, kw_ref[pl.ds(w0, wlen), :]) + heads(
        jnp.where((gap_w <= t0 - w0) & (gap_w > t0 - w0 - WINDOW), 0.0, NEG))
    e_w = jnp.exp2(s_w - jnp.max(s_w, axis=-1, keepdims=True))
    o_w = _dot(e_w.astype(bf16), vw_ref[pl.ds(w0, wlen), :]) * (1.0 / jnp.sum(e_w, axis=-1, keepdims=True))

    gap_c = (lax.broadcasted_iota(jnp.int32, (QB, ncmp), 1) * CMP_STRIDE + (CMP_LEN - 1)
             - lax.broadcasted_iota(jnp.int32, (QB, ncmp), 0))
    s_c = _dot_nt(q, kc_ref[...]) + heads(jnp.where(gap_c <= t0, 0.0, NEG))
    e_c = jnp.exp2(s_c - jnp.max(s_c, axis=-1, keepdims=True))
    inv_c = jnp.where(t_rows >= CMP_LEN - 1, 1.0, 0.0) / jnp.sum(e_c, axis=-1, keepdims=True)
    p_c = e_c * inv_c
    o_c = _dot(p_c.astype(bf16), vc_ref[...])

    psum = p_c[0:QB]
    for r in range(1, B_REP):
        psum = psum + p_c[r * QB:(r + 1) * QB]
    overlap = ov_ref[...]
    imp = jnp.zeros((QB, NBLK), f32)
    rem = psum
    for _ in range(3):
        part = rem.astype(bf16)
        imp = imp + _dot(part, overlap)
        rem = rem - part.astype(f32)
    tq = t0 + lax.broadcasted_iota(jnp.int32, (QB, NBLK), 0)
    blk = lax.broadcasted_iota(jnp.int32, (QB, NBLK), 1)
    dist = (tq >> SEL_SHIFT) - blk
    forced = (blk == 0) | ((dist >= 0) & (dist < SEL_LOCAL))
    val = jnp.where(forced, jnp.inf, jnp.where(blk * SEL_LEN <= tq, imp, -jnp.inf))
    vt = val.T
    picked_t = _topk_blocks_t(vt) & (vt > -jnp.inf)

    td = pl.multiple_of(t0, QB)
    causal_d = lax.broadcasted_iota(jnp.int32, (QB, QB), 1) <= lax.broadcasted_iota(jnp.int32, (QB, QB), 0)
    s_d = _dot_nt(q, ks_ref[pl.ds(td, QB), :]) + heads(jnp.where(causal_d, 0.0, NEG))
    m_d = jnp.max(s_d, axis=-1, keepdims=True)
    acc_d = _dot(jnp.exp2(s_d - m_d).astype(bf16), vaug_ref[pl.ds(td, QB), :])
    bsub = lax.broadcasted_iota(jnp.int32, (NBLK, QB), 0)
    bias = jnp.where(picked_t & (bsub * SEL_LEN < t0), 0.0, NEG).T.astype(bf16)
    qa = jnp.concatenate([q, jnp.concatenate([bias] * B_REP, axis=0)], axis=1)

    def sel_step(it, carry):
        m, acc = carry
        tiles = []
        for u in range(SEL_UNROLL):
            k0 = pl.multiple_of((it * SEL_UNROLL + u) * KT, KT)
            tiles.append((_dot_nt(qa, kaug_ref[pl.ds(k0, KT), :]), k0))
        for s, k0 in tiles:
            m_new = jnp.maximum(m, jnp.max(s, axis=-1, keepdims=True))
            p = jnp.exp2(s - m_new).astype(bf16)
            acc = jnp.exp2(m - m_new) * acc + _dot(p, vaug_ref[pl.ds(k0, KT), :])
            m = m_new
        return m, acc

    n_steps = (t0 + SEL_UNROLL * KT - 1) // (SEL_UNROLL * KT)
    _, acc_s = lax.fori_loop(0, n_steps, sel_step, (m_d, acc_d))
    o_s = acc_s[:, 0:B_HD] / acc_s[:, B_HD:]

    ng = 3 * B_REP
    gt = _sigmoid(pltpu.roll(gate_ref[...], (LANES - ng * pl.program_id(0)) % LANES, 1))
    for r in range(B_REP):
        rs = slice(r * QB, (r + 1) * QB)
        o = (gt[:, 3 * r:3 * r + 1] * o_c[rs] + gt[:, 3 * r + 1:3 * r + 2] * o_s[rs]
             + gt[:, 3 * r + 2:3 * r + 3] * o_w[rs])
        o_ref[:, r * B_HD:(r + 1) * B_HD] = (o * _silu(z_ref[:, r * B_HD:(r + 1) * B_HD])).astype(o_ref.dtype)


def nsa_mixer(p32, p16, cmp):
    L = p32.shape[0]
    assert L // SEL_LEN <= NBLK and L % (SEL_UNROLL * KT) == 0
    ncmp = cmp.shape[2]
    gw = B_REP * B_HD
    kv = lambda off: pl.BlockSpec((L, B_HD), lambda g, i: (0, off // B_HD + g))
    return pl.pallas_call(
        _nsa_body,
        grid=(B_KV, L // QB),
        in_specs=[pl.BlockSpec((QB, gw), lambda g, i: (i, H_QB // gw + g)),
                  pl.BlockSpec((None, None, ncmp, B_HD), lambda g, i: (0, g, 0, 0)),
                  pl.BlockSpec((None, None, ncmp, B_HD), lambda g, i: (1, g, 0, 0)),
                  kv(H_KS), kv(H_VS), kv(H_KW), kv(H_VW),
                  pl.BlockSpec((QB, LANES), lambda g, i: (i, C_GATES // LANES)),
                  pl.BlockSpec((QB, gw), lambda g, i: (i, C_ZB // gw + g))],
        out_specs=pl.BlockSpec((QB, gw), lambda g, i: (i, g)),
        out_shape=jax.ShapeDtypeStruct((L, WIDTH), bf16),
        scratch_shapes=[pltpu.VMEM((L, B_HD + NBLK), bf16), pltpu.VMEM((L, 2 * B_HD), bf16),
                        pltpu.VMEM((ncmp, NBLK), bf16)],
        compiler_params=_cparams("parallel", "arbitrary"),
        name="nsa_attention",
    )(p16, cmp, cmp, p16, p16, p16, p16, p32, p32)


def _merge_body(ya_ref, yb_ref, yc_ref, w_ref, ga_ref, gb_ref, gc_ref, o_ref):
    acc = _sigmoid(ga_ref[...]) * _dot(ya_ref[...], w_ref[0])
    acc = acc + _sigmoid(gb_ref[...]) * _dot(yb_ref[...], w_ref[1])
    acc = acc + _sigmoid(gc_ref[...]) * _dot(yc_ref[...], w_ref[2])
    o_ref[...] = acc.astype(o_ref.dtype)


def merge_branches(ya, yb, yc, w_branch, p32):
    L = ya.shape[0]
    tm, tn = min(512, L), 512
    yspec = pl.BlockSpec((tm, WIDTH), lambda j, i: (i, 0))
    gspec = lambda b: pl.BlockSpec((tm, tn), lambda j, i: (i, (C_GM + b * D_MODEL) // tn + j))
    return pl.pallas_call(
        _merge_body,
        grid=(D_MODEL // tn, L // tm),
        in_specs=[yspec, yspec, yspec,
                  pl.BlockSpec((3, WIDTH, tn), lambda j, i: (0, 0, j)),
                  gspec(0), gspec(1), gspec(2)],
        out_specs=pl.BlockSpec((tm, tn), lambda j, i: (i, j)),
        out_shape=jax.ShapeDtypeStruct((L, D_MODEL), bf16),
        compiler_params=_cparams("parallel", "parallel"),
        name="branch_merge",
    )(ya, yb, yc, w_branch.astype(bf16), p32, p32, p32)


def _rms(x):
    return x * lax.rsqrt(jnp.mean(x * x, axis=-1, keepdims=True) + EPS)


def _post_body(x_ref, m_ref, p_ref, wo_ref, wp_ref, wg_ref, npost_ref, nple_ref, o_ref):
    x1 = x_ref[...] + _rms(_dot(m_ref[...], wo_ref[...])) * npost_ref[...]
    e = _dot(p_ref[...].astype(bf16), wp_ref[...])
    g = _sigmoid(_dot(_rms(x1).astype(bf16), wg_ref[...]))
    o_ref[...] = x1 + _rms(g * e) * nple_ref[...]


def post_mixer(x, merged, p, w_out, ple_proj, ple_gate, norm_post, ple_norm):
    L = x.shape[0]
    tm = min(256, L)
    const = lambda shape: pl.BlockSpec(shape, lambda i: (0,) * len(shape))
    return pl.pallas_call(
        _post_body,
        grid=(L // tm,),
        in_specs=[pl.BlockSpec((tm, D_MODEL), lambda i: (i, 0)),
                  pl.BlockSpec((tm, D_MODEL), lambda i: (i, 0)),
                  pl.BlockSpec((tm, P_DIM), lambda i: (i, 0)),
                  const((D_MODEL, D_MODEL)), const((P_DIM, D_MODEL)), const((D_MODEL, D_MODEL)),
                  const((1, D_MODEL)), const((1, D_MODEL))],
        out_specs=pl.BlockSpec((tm, D_MODEL), lambda i: (i, 0)),
        out_shape=jax.ShapeDtypeStruct((L, D_MODEL), f32),
        compiler_params=_cparams("parallel"),
        name="post_mixer",
    )(x, merged, p, w_out.astype(bf16), ple_proj.astype(bf16), ple_gate.astype(bf16),
      norm_post.reshape(1, D_MODEL), ple_norm.reshape(1, D_MODEL))


def _pack_sources():
    names = ("xa", "za", "ba", "ca", "dta", "qb", "kcb", "vcb", "ksb", "vsb", "kwb", "vwb", "gb", "zb",
             "qc", "kc", "vc", "zc", "gm")
    off = dict(zip(names, [0] + np.cumsum(IN_SIZES)[:-1].tolist()))
    size = dict(zip(names, IN_SIZES))
    tiles = lambda *sec: [(off[s] + PACK_T * t, int(s in ("qc", "kc"))) for s in sec for t in range(size[s] // PACK_T)]
    src32 = tiles("xa", "ba", "ca", "za", "qc", "kc", "zc", "zb", "gm", "kcb", "vcb") + [(off["gb"], 0), (off["dta"], 0)]
    src16 = tiles("qb", "vc", "ksb", "vsb", "kwb", "vwb")
    assert len(src32) * PACK_T == N32 and len(src16) * PACK_T == N16
    return src32, src16


def layer(i, x, p, norm_pre, norm_post, w_in_t, conv_w, conv_b, dt_bias, a_log, d_skip, ssm_norm,
          cmp_pe, cmp_w1, cmp_w2, ret_norm, w_branch, w_out, ple_proj, ple_gate, ple_norm):
    src32, src16 = _pack_sources()
    h = pre_norm(x, norm_pre)
    p32 = in_proj(h, w_in_t, i, src32, f32, "in_proj_f32")
    p16 = in_proj(h, w_in_t, i, src16, bf16, "in_proj_bf16")
    ya = ssd_mixer(p32, conv_w, conv_b, dt_bias, a_log, d_skip, ssm_norm)
    cmp = compress_kv(p32, cmp_pe, cmp_w1, cmp_w2)
    yb = nsa_mixer(p32, p16, cmp)
    yc = retention_mixer(p32, p16, ret_norm)
    merged = merge_branches(ya, yb, yc, w_branch, p32)
    return post_mixer(x, merged, p, w_out, ple_proj, ple_gate, norm_post, ple_norm)


def kernel(x, p, norm_pre, norm_post, w_in, conv_w, conv_b, dt_bias, a_log, d_skip, ssm_norm, cmp_pe, cmp_w1,
           cmp_w2, ret_norm, w_branch, w_out, ple_proj, ple_gate, ple_norm):
    b, L, d = x.shape
    assert b == 1 and d == D_MODEL
    xf = x.reshape(L, d)
    w_in_t = jnp.swapaxes(w_in, 1, 2)
    for i in range(p.shape[0]):
        xf = layer(i, xf, p[i, 0], norm_pre[i], norm_post[i], w_in_t, conv_w[i], conv_b[i], dt_bias[i], a_log[i],
                   d_skip[i], ssm_norm[i], cmp_pe[i], cmp_w1[i], cmp_w2[i], ret_norm[i], w_branch[i], w_out[i],
                   ple_proj[i], ple_gate[i], ple_norm[i])
    return xf.reshape(b, L, d)
```

```python
import functools

import jax
import jax.numpy as jnp
import numpy as np
from jax import lax
from jax.experimental import pallas as pl
from jax.experimental.pallas import tpu as pltpu

f32 = jnp.float32
bf16 = jnp.bfloat16

D_MODEL = 2048
P_DIM = 256
EPS = 1e-6
NEG = -1e30
WIDTH = 2048

A_HEADS, A_HEAD_DIM, A_GROUPS, A_STATE, A_CONV, CHUNK = 32, 64, 8, 128, 4, 128
A_GW = (A_HEADS // A_GROUPS) * A_HEAD_DIM
A_CONV_CH = WIDTH + 2 * A_GROUPS * A_STATE
B_HEADS, B_KV, B_HD = 16, 4, 128
B_REP = B_HEADS // B_KV
CMP_LEN, CMP_STRIDE, CMP_HIDDEN = 32, 16, 256
SEL_LEN, SEL_TOPK, SEL_LOCAL, WINDOW, QB = 64, 16, 2, 512, 128
SEL_SHIFT = SEL_LEN.bit_length() - 1
assert SEL_LOCAL * SEL_LEN >= QB and QB % SEL_LEN == 0
LOG2E = 1.4426950408889634
KT = 1024
SEL_UNROLL = 2
C_HEADS, C_HD = 8, 256
ROPE_BASE = 10000.0

IN_SIZES = (2048, 2048, 1024, 1024, 32,
            2048, 512, 512, 512, 512, 512, 512, 48, 2048,
            2048, 2048, 2048, 2048, 6144)

C_XBC, C_ZA, C_QC, C_KC, C_ZC, C_ZB, C_GM = 0, 4096, 6144, 8192, 10240, 12288, 14336
C_KCB, C_VCB, C_GATES, C_DT, N32 = 20480, 20992, 21504, 22016, 22528
H_QB, H_VC, H_KS, H_VS, H_KW, H_VW, N16 = 0, 2048, 4096, 4608, 5120, 5632, 6144

LANES = 128
VMEM_LIMIT = 56 * 1024 * 1024


def _cparams(*sem):
    return pltpu.CompilerParams(dimension_semantics=sem, vmem_limit_bytes=VMEM_LIMIT)


def _sigmoid(x):
    return 1.0 / (1.0 + jnp.exp(-x))


def _silu(x):
    return x * _sigmoid(x)


def _dot(a, b):
    return jnp.dot(a, b, preferred_element_type=f32)


def _dot_nt(a, b):
    return lax.dot_general(a, b, (((1,), (1,)), ((), ())), preferred_element_type=f32)


def _dot_tn(a, b):
    return lax.dot_general(a, b, (((0,), (0,)), ((), ())), preferred_element_type=f32)


def _norm_matmul_body(x_ref, g_ref, w_ref, o_ref, h_ref):
    @pl.when(pl.program_id(1) == 0)
    def _():
        xf = x_ref[...]
        ms = jnp.mean(xf * xf, axis=-1, keepdims=True)
        h_ref[...] = (xf * lax.rsqrt(ms + EPS) * g_ref[...]).astype(bf16)

    o_ref[...] = _dot_nt(h_ref[...], w_ref[...]).astype(o_ref.dtype)


def norm_matmul(x, gain, w_t, out_dtype, name):
    m, k = x.shape
    n = w_t.shape[0]
    tm, tn = min(1024, m), 1024
    return pl.pallas_call(
        _norm_matmul_body,
        grid=(m // tm, n // tn),
        in_specs=[pl.BlockSpec((tm, k), lambda i, j: (i, 0)),
                  pl.BlockSpec((1, k), lambda i, j: (0, 0)),
                  pl.BlockSpec((tn, k), lambda i, j: (j, 0))],
        out_specs=pl.BlockSpec((tm, tn), lambda i, j: (i, j)),
        out_shape=jax.ShapeDtypeStruct((m, n), out_dtype),
        scratch_shapes=[pltpu.VMEM((tm, k), bf16)],
        compiler_params=_cparams("parallel", "arbitrary"),
        name=name,
    )(x, gain.reshape(1, k), w_t)


def _expand_heads(v, g, width):
    rows = v.shape[0]
    hd = width // 4
    lane = lax.broadcasted_iota(jnp.int32, (rows, width), 1)
    out = jnp.broadcast_to(v[:, 4 * g + 3:4 * g + 4], (rows, width))
    for r in (2, 1, 0):
        out = jnp.where(lane < (r + 1) * hd, jnp.broadcast_to(v[:, 4 * g + r:4 * g + r + 1], (rows, width)), out)
    return out


def _ssd_body(xbc_ref, z_ref, dt_ref, cw_ref, cb_ref, dtb_ref, alog_ref, dskip_ref, nw_ref,
              o_ref, tail_ref, st_ref, y_ref):
    c = pl.program_id(0)

    @pl.when(c == 0)
    def _():
        tail_ref[0:SUB, :] = jnp.zeros((SUB, A_CONV_CH), f32)
        st_ref[...] = jnp.zeros_like(st_ref)

    cur = xbc_ref[...]
    tail_ref[SUB:SUB + CHUNK, :] = cur
    acc = cur * cw_ref[A_CONV - 1:A_CONV, :] + cb_ref[...]
    for s in range(1, A_CONV):
        acc = acc + tail_ref[SUB - s:SUB - s + CHUNK, :] * cw_ref[A_CONV - 1 - s:A_CONV - s, :]
    tail_ref[0:SUB, :] = cur[CHUNK - SUB:CHUNK]
    xbc = _silu(acc)
    xs = xbc[:, 0:WIDTH]
    bm = xbc[:, WIDTH:WIDTH + A_GROUPS * A_STATE]
    cm = xbc[:, WIDTH + A_GROUPS * A_STATE:]

    xdt = dt_ref[...] + dtb_ref[...]
    dt = jnp.maximum(xdt, 0.0) + jnp.log1p(jnp.exp(-jnp.abs(xdt)))
    a = -jnp.exp(alog_ref[...]) * dt
    rowi = lax.broadcasted_iota(jnp.int32, (CHUNK, LANES), 0)
    a_cs = a
    s = 1
    while s < CHUNK:
        a_cs = a_cs + jnp.where(rowi >= s, pltpu.roll(a_cs, s, 0), 0.0)
        s *= 2
    a_cs_t = a_cs.T
    a_last = a_cs[CHUNK - 1:CHUNK, :]
    exp_acs = jnp.exp(a_cs)
    dend = jnp.exp(a_last - a_cs)
    cdecay = jnp.exp(a_last)

    li = lax.broadcasted_iota(jnp.int32, (CHUNK, CHUNK), 0)
    si = lax.broadcasted_iota(jnp.int32, (CHUNK, CHUNK), 1)
    causal = li >= si
    lane_g = lax.broadcasted_iota(jnp.int32, (CHUNK, A_GW), 1)

    for g in range(A_GROUPS):
        cm_g = cm[:, g * A_STATE:(g + 1) * A_STATE].astype(bf16)
        bm_g = bm[:, g * A_STATE:(g + 1) * A_STATE].astype(bf16)
        xs_g = xs[:, g * A_GW:(g + 1) * A_GW]
        cb = _dot_nt(cm_g, bm_g)
        xdt_g = xs_g * _expand_heads(dt, g, A_GW)
        xdt_b = xdt_g.astype(bf16)
        lhs, rhs = [], []
        for r in range(4):
            h = 4 * g + r
            seg = a_cs[:, h:h + 1] - a_cs_t[h:h + 1, :]
            dec = jnp.exp(jnp.where(causal, seg, -jnp.inf))
            lhs.append((cb * dec).astype(bf16))
            keep = (lane_g >= r * A_HEAD_DIM) & (lane_g < (r + 1) * A_HEAD_DIM)
            rhs.append(jnp.where(keep, xdt_b, jnp.zeros_like(xdt_b)))
        y_diag = _dot(jnp.concatenate(lhs, axis=1), jnp.concatenate(rhs, axis=0))
        st = st_ref[g]
        y_off = _dot(cm_g, st.astype(bf16)) * _expand_heads(exp_acs, g, A_GW)
        y_ref[:, g * A_GW:(g + 1) * A_GW] = y_diag + y_off + xs_g * dskip_ref[:, g * A_GW:(g + 1) * A_GW]
        wgt = (xdt_g * _expand_heads(dend, g, A_GW)).astype(bf16)
        st_ref[g] = st * _expand_heads(cdecay, g, A_GW) + _dot_tn(bm_g, wgt)

    yz = y_ref[...] * _silu(z_ref[...])
    ms = jnp.mean(yz * yz, axis=-1, keepdims=True)
    o_ref[...] = (yz * lax.rsqrt(ms + EPS) * nw_ref[...]).astype(o_ref.dtype)


def ssd_mixer(p32, conv_w, conv_b, dt_bias, a_log, d_skip, norm_w):
    L = p32.shape[0]
    pad = LANES - A_HEADS
    dtb = jnp.pad(dt_bias, (0, pad)).reshape(1, LANES)
    alog = jnp.pad(a_log, (0, pad)).reshape(1, LANES)
    dskip = jnp.repeat(d_skip, A_HEAD_DIM).reshape(1, WIDTH)
    const = lambda shape: pl.BlockSpec(shape, lambda c: (0,) * len(shape))
    return pl.pallas_call(
        _ssd_body,
        grid=(L // CHUNK,),
        in_specs=[pl.BlockSpec((CHUNK, A_CONV_CH), lambda c: (c, C_XBC // A_CONV_CH)),
                  pl.BlockSpec((CHUNK, WIDTH), lambda c: (c, C_ZA // WIDTH)),
                  pl.BlockSpec((CHUNK, LANES), lambda c: (c, C_DT // LANES)),
                  const((A_CONV, A_CONV_CH)), const((1, A_CONV_CH)), const((1, LANES)), const((1, LANES)),
                  const((1, WIDTH)), const((1, WIDTH))],
        out_specs=pl.BlockSpec((CHUNK, WIDTH), lambda c: (c, 0)),
        out_shape=jax.ShapeDtypeStruct((L, WIDTH), bf16),
        scratch_shapes=[pltpu.VMEM((SUB + CHUNK, A_CONV_CH), f32),
                        pltpu.VMEM((A_GROUPS, A_STATE, A_GW), f32),
                        pltpu.VMEM((CHUNK, WIDTH), f32)],
        compiler_params=_cparams("arbitrary"),
        name="ssd_mixer",
    )(p32, p32, p32, conv_w, conv_b.reshape(1, A_CONV_CH), dtb, alog, dskip, norm_w.reshape(1, WIDTH))


def _ret_body(q_ref, k_ref, z_ref, v_ref, cos_ref, sin_ref, dmat_ref, zeta_ref, xi_ref, cd_ref, gn_ref,
              o_ref, r_ref):
    c = pl.program_id(0)

    @pl.when(c == 0)
    def _():
        r_ref[...] = jnp.zeros_like(r_ref)

    cos = cos_ref[...]
    sin = sin_ref[...]
    half = C_HD // 2

    def rot(x):
        x1, x2 = x[:, :half], x[:, half:]
        return jnp.concatenate([x1 * cos - x2 * sin, x1 * sin + x2 * cos], axis=1)

    for h in range(C_HEADS):
        sl = slice(h * C_HD, (h + 1) * C_HD)
        qr = rot(q_ref[:, sl])
        kr = rot(k_ref[:, sl]) * (C_HD ** -0.5)
        qb, kb = qr.astype(bf16), kr.astype(bf16)
        v = v_ref[:, sl]
        s = _dot_nt(qb, kb) * dmat_ref[h]
        inner = _dot(s.astype(bf16), v)
        rst = r_ref[h]
        cross = _dot(qb, rst.astype(bf16)) * xi_ref[:, h:h + 1]
        o = inner + cross
        mu = jnp.mean(o, axis=-1, keepdims=True)
        d = o - mu
        var = jnp.mean(d * d, axis=-1, keepdims=True)
        on = d * lax.rsqrt(var + EPS) * gn_ref[:, sl]
        o_ref[:, sl] = (_silu(z_ref[:, sl]) * on).astype(o_ref.dtype)
        kz = (kr * zeta_ref[:, h:h + 1]).astype(bf16)
        r_ref[h] = rst * cd_ref[0:1, h:h + 1] + _dot_tn(kz, v)


def _retention_tables(L):
    T, H, d = CHUNK, C_HEADS, C_HD
    pos = jnp.arange(L)
    inv = ROPE_BASE ** (-jnp.arange(0, d, 2, dtype=f32) / d)
    ang = pos.astype(f32)[:, None] * inv[None, :]
    log_g = jnp.log1p(-jnp.exp2(-5.0 - jnp.arange(H, dtype=f32)))
    i = jnp.arange(T, dtype=f32)
    diff = i[:, None] - i[None, :]
    dmat = jnp.where(diff >= 0, jnp.exp(log_g[:, None, None] * jnp.maximum(diff, 0.0)), 0.0)
    zeta = jnp.exp(log_g[:, None] * (T - 1 - i)[None, :])
    xi = jnp.exp(log_g[:, None] * (i + 1.0)[None, :])
    cdecay = jnp.exp(log_g * T)
    padl = lambda t: jnp.pad(t, ((0, 0), (0, LANES - H)))
    return (jnp.cos(ang), jnp.sin(ang), dmat, padl(zeta.T), padl(xi.T), padl(cdecay[None, :]))


def retention_mixer(p32, p16, gn_w):
    L = p32.shape[0]
    cos, sin, dmat, zeta_t, xi_t, cdecay = _retention_tables(L)
    const = lambda shape: pl.BlockSpec(shape, lambda c: (0,) * len(shape))
    return pl.pallas_call(
        _ret_body,
        grid=(L // CHUNK,),
        in_specs=[pl.BlockSpec((CHUNK, WIDTH), lambda c: (c, C_QC // WIDTH)),
                  pl.BlockSpec((CHUNK, WIDTH), lambda c: (c, C_KC // WIDTH)),
                  pl.BlockSpec((CHUNK, WIDTH), lambda c: (c, C_ZC // WIDTH)),
                  pl.BlockSpec((CHUNK, WIDTH), lambda c: (c, H_VC // WIDTH)),
                  pl.BlockSpec((CHUNK, C_HD // 2), lambda c: (c, 0)),
                  pl.BlockSpec((CHUNK, C_HD // 2), lambda c: (c, 0)),
                  const((C_HEADS, CHUNK, CHUNK)), const((CHUNK, LANES)), const((CHUNK, LANES)),
                  const((1, LANES)), const((1, WIDTH))],
        out_specs=pl.BlockSpec((CHUNK, WIDTH), lambda c: (c, 0)),
        out_shape=jax.ShapeDtypeStruct((L, WIDTH), bf16),
        scratch_shapes=[pltpu.VMEM((C_HEADS, C_HD, C_HD), f32)],
        compiler_params=_cparams("arbitrary"),
        name="retention_mixer",
    )(p32, p32, p32, p16, cos, sin, dmat, zeta_t, xi_t, cdecay, gn_w.reshape(1, WIDTH))


def _compress_body(x_ref, pe_ref, w1_ref, w2_ref, o_ref):
    n = o_ref.shape[0]
    acc = [jnp.zeros((n, CMP_HIDDEN), f32) for _ in range(2)]
    for b in range(CMP_STRIDE):
        x = x_ref[pl.ds(b, n, stride=CMP_STRIDE), :]
        for a in range(2):
            acc[a] = acc[a] + _dot((x + pe_ref[a, b]).astype(bf16), w1_ref[a, b])
    pre = acc[0] + pltpu.roll(acc[1], n - 1, 0)
    out = _dot(_silu(pre).astype(bf16), w2_ref[...])
    row = lax.broadcasted_iota(jnp.int32, out.shape, 0)
    o_ref[...] = jnp.where(row < n - 1, out, 0.0).astype(o_ref.dtype)


def compress_kv(p32, cmp_pe, cmp_w1, cmp_w2):
    L = p32.shape[0]
    nblk = L // CMP_STRIDE
    w1 = cmp_w1.reshape(2, 2, CMP_STRIDE, B_HD, CMP_HIDDEN).astype(bf16)
    pe = cmp_pe.reshape(2, 2, CMP_STRIDE, 1, B_HD)
    w2 = cmp_w2.astype(bf16)
    return pl.pallas_call(
        _compress_body,
        grid=(2, B_KV),
        in_specs=[pl.BlockSpec((L, LANES), lambda w, h: (0, C_KCB // LANES + w * B_KV + h)),
                  pl.BlockSpec((None, 2, CMP_STRIDE, 1, B_HD), lambda w, h: (w, 0, 0, 0, 0)),
                  pl.BlockSpec((None, 2, CMP_STRIDE, B_HD, CMP_HIDDEN), lambda w, h: (w, 0, 0, 0, 0)),
                  pl.BlockSpec((None, CMP_HIDDEN, B_HD), lambda w, h: (w, 0, 0))],
        out_specs=pl.BlockSpec((None, None, nblk, B_HD), lambda w, h: (w, h, 0, 0)),
        out_shape=jax.ShapeDtypeStruct((2, B_KV, nblk, B_HD), bf16),
        compiler_params=_cparams("parallel", "parallel"),
        name="nsa_compress",
    )(p32, pe, w1, w2)


def _sort16_network():
    n, pairs, p = 16, [], 1
    while p < n:
        k = p
        while k >= 1:
            for j in range(k % p, n - k, 2 * k):
                for i in range(min(k, n - j - k)):
                    if (i + j) // (2 * p) == (i + j + k) // (2 * p):
                        pairs.append((i + j, i + j + k))
            k //= 2
        p *= 2
    return pairs


SUB = 8
NV = SEL_TOPK
NBLK = SUB * NV
assert NV == 16 and NBLK == LANES


def _topk_blocks_t(vt):
    v = [vt[SUB * a:SUB * (a + 1)] for a in range(NV)]
    s = list(v)
    for a, b in _sort16_network():
        s[a], s[b] = jnp.maximum(s[a], s[b]), jnp.minimum(s[a], s[b])
    for shift in (4, 2):
        s = [jnp.maximum(s[k], pltpu.roll(s[NV - 1 - k], shift, 0)) for k in range(NV)]
        d = NV // 2
        while d >= 1:
            for k in range(NV):
                if not k & d:
                    s[k], s[k + d] = jnp.maximum(s[k], s[k + d]), jnp.minimum(s[k], s[k + d])
            d //= 2
    top = [jnp.maximum(s[k], pltpu.roll(s[NV - 1 - k], 1, 0)) for k in range(NV)]
    thr = top[0]
    for k in range(1, NV):
        thr = jnp.minimum(thr, top[k])
    sub = lax.broadcasted_iota(jnp.int32, v[0].shape, 0)
    gt = [jnp.where(x > thr, 1.0, 0.0) for x in v]
    eq = [jnp.where(x == thr, 1.0, 0.0) for x in v]

    def sub_scan(x):
        for sh in (1, 2, 4):
            x = x + jnp.where(sub >= sh, pltpu.roll(x, sh, 0), 0.0)
        return x

    n_gt = gt[0]
    for a in range(1, NV):
        n_gt = n_gt + gt[a]
    n_gt = sub_scan(n_gt)
    need = float(SEL_TOPK) - jnp.broadcast_to(n_gt[SUB - 1:SUB], n_gt.shape)
    out, before = [], jnp.zeros_like(thr)
    for a in range(NV):
        inc = sub_scan(eq[a])
        out.append((gt[a] > 0.5) | ((eq[a] > 0.5) & (before + inc - eq[a] < need)))
        before = before + jnp.broadcast_to(inc[SUB - 1:SUB], inc.shape)
    return jnp.concatenate(out, axis=0)


def _nsa_body(q_ref, kc_ref, vc_ref, ks_ref, vs_ref, kw_ref, vw_ref, gate_ref, z_ref, o_ref,
              kaug_ref, vaug_ref, ov_ref):
    i = pl.program_id(1)
    L = ks_ref.shape[0]
    ncmp = kc_ref.shape[0]
    t0 = i * QB
    rows = B_REP * QB

    @pl.when(i == 0)
    def _():
        kaug_ref[:, 0:B_HD] = ks_ref[...]
        vaug_ref[:, 0:B_HD] = vs_ref[...]
        vaug_ref[:, B_HD:] = jnp.ones((L, B_HD), bf16)
        for c in range(L // KT):
            u = c * KT + lax.broadcasted_iota(jnp.int32, (KT, NBLK), 0)
            j = lax.broadcasted_iota(jnp.int32, (KT, NBLK), 1)
            kaug_ref[c * KT:(c + 1) * KT, B_HD:] = jnp.where(j == (u >> SEL_SHIFT), 1.0, 0.0).astype(bf16)
        ci = lax.broadcasted_iota(jnp.int32, (ncmp, NBLK), 0)
        sj = lax.broadcasted_iota(jnp.int32, (ncmp, NBLK), 1)
        ov_ref[...] = jnp.where((ci * CMP_STRIDE < (sj + 1) * SEL_LEN) & (ci * CMP_STRIDE + CMP_LEN > sj * SEL_LEN),
                                1.0, 0.0).astype(bf16)

    q = jnp.concatenate([q_ref[:, r * B_HD:(r + 1) * B_HD] for r in range(B_REP)], axis=0)
    q = (q.astype(f32) * (B_HD ** -0.5 * LOG2E)).astype(bf16)
    t_rows = t0 + (lax.broadcasted_iota(jnp.int32, (rows, 1), 0) & (QB - 1))
    t_tok = t0 + lax.broadcasted_iota(jnp.int32, (QB, 1), 0)

    def heads(x):
        return jnp.concatenate([x] * B_REP, axis=0)

    n_idx = lax.broadcasted_iota(jnp.int32, (QB, ncmp), 1)
    wlen = WINDOW + QB
    w0 = pl.multiple_of(jnp.maximum(t0 - WINDOW, 0), QB)
    td = pl.multiple_of(t0, QB)
    s_all = _dot_nt(q, jnp.concatenate([kc_ref[...], kw_ref[pl.ds(w0, wlen), :], ks_ref[pl.ds(td, QB), :]], axis=0))
    s_c = s_all[:, 0:ncmp] + heads(jnp.where(n_idx * CMP_STRIDE + (CMP_LEN - 1) <= t_tok, 0.0, NEG))
    e_c = jnp.exp2(s_c - jnp.max(s_c, axis=-1, keepdims=True))
    inv_c = jnp.where(t_rows >= CMP_LEN - 1, 1.0, 0.0) / jnp.sum(e_c, axis=-1, keepdims=True)
    p_c = e_c * inv_c
    o_c = _dot(p_c.astype(bf16), vc_ref[...])

    psum = p_c[0:QB]
    for r in range(1, B_REP):
        psum = psum + p_c[r * QB:(r + 1) * QB]
    overlap = ov_ref[...]
    imp = jnp.zeros((QB, NBLK), f32)
    rem = psum
    for _ in range(3):
        part = rem.astype(bf16)
        imp = imp + _dot(part, overlap)
        rem = rem - part.astype(f32)
    tq = t0 + lax.broadcasted_iota(jnp.int32, (QB, NBLK), 0)
    blk = lax.broadcasted_iota(jnp.int32, (QB, NBLK), 1)
    dist = (tq >> SEL_SHIFT) - blk
    forced = (blk == 0) | ((dist >= 0) & (dist < SEL_LOCAL))
    val = jnp.where(forced, jnp.inf, jnp.where(blk * SEL_LEN <= tq, imp, -jnp.inf))
    vt = val.T
    picked_t = _topk_blocks_t(vt) & (vt > -jnp.inf)

    lane_d = lax.broadcasted_iota(jnp.int32, (QB, QB), 1)
    s_d = s_all[:, ncmp + wlen:] + heads(jnp.where(t0 + lane_d <= t_tok, 0.0, NEG))
    m_d = jnp.max(s_d, axis=-1, keepdims=True)
    acc_d = _dot(jnp.exp2(s_d - m_d).astype(bf16), vaug_ref[pl.ds(td, QB), :])
    bsub = lax.broadcasted_iota(jnp.int32, (NBLK, QB), 0)
    bias = jnp.where(picked_t & (bsub * SEL_LEN < t0), 0.0, NEG).T.astype(bf16)
    qa = jnp.concatenate([q, jnp.concatenate([bias] * B_REP, axis=0)], axis=1)

    def sel_step(it, carry):
        m, acc = carry
        tiles = []
        for u in range(SEL_UNROLL):
            k0 = pl.multiple_of((it * SEL_UNROLL + u) * KT, KT)
            tiles.append((_dot_nt(qa, kaug_ref[pl.ds(k0, KT), :]), k0))
        for s, k0 in tiles:
            m_new = jnp.maximum(m, jnp.max(s, axis=-1, keepdims=True))
            p = jnp.exp2(s - m_new).astype(bf16)
            acc = jnp.exp2(m - m_new) * acc + _dot(p, vaug_ref[pl.ds(k0, KT), :])
            m = m_new
        return m, acc

    n_steps = (t0 + SEL_UNROLL * KT - 1) // (SEL_UNROLL * KT)
    _, acc_s = lax.fori_loop(0, n_steps, sel_step, (m_d, acc_d))
    o_s = acc_s[:, 0:B_HD] / acc_s[:, B_HD:]

    kpos = w0 + lax.broadcasted_iota(jnp.int32, (QB, wlen), 1)
    s_w = s_all[:, ncmp:ncmp + wlen] + heads(
        jnp.where((kpos <= t_tok) & (kpos > t_tok - WINDOW), 0.0, NEG))
    e_w = jnp.exp2(s_w - jnp.max(s_w, axis=-1, keepdims=True))
    o_w = _dot(e_w.astype(bf16), vw_ref[pl.ds(w0, wlen), :]) * (1.0 / jnp.sum(e_w, axis=-1, keepdims=True))

    ng = 3 * B_REP
    gt = _sigmoid(pltpu.roll(gate_ref[...], (LANES - ng * pl.program_id(0)) % LANES, 1))
    for r in range(B_REP):
        rs = slice(r * QB, (r + 1) * QB)
        o = (gt[:, 3 * r:3 * r + 1] * o_c[rs] + gt[:, 3 * r + 1:3 * r + 2] * o_s[rs]
             + gt[:, 3 * r + 2:3 * r + 3] * o_w[rs])
        o_ref[:, r * B_HD:(r + 1) * B_HD] = (o * _silu(z_ref[:, r * B_HD:(r + 1) * B_HD])).astype(o_ref.dtype)


def nsa_mixer(p32, p16, cmp):
    L = p32.shape[0]
    assert L // SEL_LEN <= NBLK and L % (SEL_UNROLL * KT) == 0
    ncmp = cmp.shape[2]
    gw = B_REP * B_HD
    kv = lambda off: pl.BlockSpec((L, B_HD), lambda g, i: (0, off // B_HD + g))
    return pl.pallas_call(
        _nsa_body,
        grid=(B_KV, L // QB),
        in_specs=[pl.BlockSpec((QB, gw), lambda g, i: (i, H_QB // gw + g)),
                  pl.BlockSpec((None, None, ncmp, B_HD), lambda g, i: (0, g, 0, 0)),
                  pl.BlockSpec((None, None, ncmp, B_HD), lambda g, i: (1, g, 0, 0)),
                  kv(H_KS), kv(H_VS), kv(H_KW), kv(H_VW),
                  pl.BlockSpec((QB, LANES), lambda g, i: (i, C_GATES // LANES)),
                  pl.BlockSpec((QB, gw), lambda g, i: (i, C_ZB // gw + g))],
        out_specs=pl.BlockSpec((QB, gw), lambda g, i: (i, g)),
        out_shape=jax.ShapeDtypeStruct((L, WIDTH), bf16),
        scratch_shapes=[pltpu.VMEM((L, B_HD + NBLK), bf16), pltpu.VMEM((L, 2 * B_HD), bf16),
                        pltpu.VMEM((ncmp, NBLK), bf16)],
        compiler_params=_cparams("parallel", "arbitrary"),
        name="nsa_attention",
    )(p16, cmp, cmp, p16, p16, p16, p16, p32, p32)


def _merge_body(ya_ref, yb_ref, yc_ref, w_ref, ga_ref, gb_ref, gc_ref, o_ref):
    acc = _sigmoid(ga_ref[...]) * _dot(ya_ref[...], w_ref[0])
    acc = acc + _sigmoid(gb_ref[...]) * _dot(yb_ref[...], w_ref[1])
    acc = acc + _sigmoid(gc_ref[...]) * _dot(yc_ref[...], w_ref[2])
    o_ref[...] = acc.astype(o_ref.dtype)


def merge_branches(ya, yb, yc, w_branch, p32):
    L = ya.shape[0]
    tm, tn = min(512, L), 512
    yspec = pl.BlockSpec((tm, WIDTH), lambda j, i: (i, 0))
    gspec = lambda b: pl.BlockSpec((tm, tn), lambda j, i: (i, (C_GM + b * D_MODEL) // tn + j))
    return pl.pallas_call(
        _merge_body,
        grid=(D_MODEL // tn, L // tm),
        in_specs=[yspec, yspec, yspec,
                  pl.BlockSpec((3, WIDTH, tn), lambda j, i: (0, 0, j)),
                  gspec(0), gspec(1), gspec(2)],
        out_specs=pl.BlockSpec((tm, tn), lambda j, i: (i, j)),
        out_shape=jax.ShapeDtypeStruct((L, D_MODEL), bf16),
        compiler_params=_cparams("parallel", "parallel"),
        name="branch_merge",
    )(ya, yb, yc, w_branch.astype(bf16), p32, p32, p32)


def _rms(x):
    return x * lax.rsqrt(jnp.mean(x * x, axis=-1, keepdims=True) + EPS)


def _post_body(x_ref, m_ref, p_ref, wo_ref, wp_ref, wg_ref, npost_ref, nple_ref, o_ref):
    x1 = x_ref[...] + _rms(_dot(m_ref[...], wo_ref[...])) * npost_ref[...]
    e = _dot(p_ref[...].astype(bf16), wp_ref[...])
    g = _sigmoid(_dot(_rms(x1).astype(bf16), wg_ref[...]))
    o_ref[...] = x1 + _rms(g * e) * nple_ref[...]


def post_mixer(x, merged, p, w_out, ple_proj, ple_gate, norm_post, ple_norm):
    L = x.shape[0]
    tm = min(256, L)
    const = lambda shape: pl.BlockSpec(shape, lambda i: (0,) * len(shape))
    return pl.pallas_call(
        _post_body,
        grid=(L // tm,),
        in_specs=[pl.BlockSpec((tm, D_MODEL), lambda i: (i, 0)),
                  pl.BlockSpec((tm, D_MODEL), lambda i: (i, 0)),
                  pl.BlockSpec((tm, P_DIM), lambda i: (i, 0)),
                  const((D_MODEL, D_MODEL)), const((P_DIM, D_MODEL)), const((D_MODEL, D_MODEL)),
                  const((1, D_MODEL)), const((1, D_MODEL))],
        out_specs=pl.BlockSpec((tm, D_MODEL), lambda i: (i, 0)),
        out_shape=jax.ShapeDtypeStruct((L, D_MODEL), f32),
        compiler_params=_cparams("parallel"),
        name="post_mixer",
    )(x, merged, p, w_out.astype(bf16), ple_proj.astype(bf16), ple_gate.astype(bf16),
      norm_post.reshape(1, D_MODEL), ple_norm.reshape(1, D_MODEL))


PACK_T = 512


def _pack_sources():
    names = ("xa", "za", "ba", "ca", "dta", "qb", "kcb", "vcb", "ksb", "vsb", "kwb", "vwb", "gb", "zb",
             "qc", "kc", "vc", "zc", "gm")
    off = dict(zip(names, [0] + np.cumsum(IN_SIZES)[:-1].tolist()))
    size = dict(zip(names, IN_SIZES))
    tiles = lambda *sec: [(off[s] + PACK_T * t, int(s in ("qc", "kc"))) for s in sec for t in range(size[s] // PACK_T)]
    src32 = tiles("xa", "ba", "ca", "za", "qc", "kc", "zc", "zb", "gm", "kcb", "vcb") + [(off["gb"], 0), (off["dta"], 0)]
    src16 = tiles("qb", "vc", "ksb", "vsb", "kwb", "vwb")
    assert len(src32) * PACK_T == N32 and len(src16) * PACK_T == N16
    return src32, src16


def _pack_body(src_ref, split_ref, w_ref, o_ref):
    j = pl.program_id(0)

    @pl.when(split_ref[j] == 0)
    def _():
        o_ref[...] = w_ref[0].astype(o_ref.dtype)

    @pl.when(split_ref[j] != 0)
    def _():
        half = C_HD // 2
        r = lax.broadcasted_iota(jnp.int32, (PACK_T, PACK_T), 0)
        c = lax.broadcasted_iota(jnp.int32, (PACK_T, PACK_T), 1)
        hs = half.bit_length() - 1
        src_row = ((r >> (hs + 1)) << (hs + 1)) + 2 * (r & (half - 1)) + ((r >> hs) & 1)
        perm = jnp.where(c == src_row, 1.0, 0.0).astype(bf16)
        o_ref[...] = _dot(perm, w_ref[0].astype(bf16)).astype(o_ref.dtype)


def pack_w_in(w_in_t, layer_idx, src):
    k = w_in_t.shape[2]
    assert all(s % SUB == 0 for s, _ in src)
    grid_spec = pltpu.PrefetchScalarGridSpec(
        num_scalar_prefetch=2, grid=(len(src),),
        in_specs=[pl.BlockSpec((pl.Element(1), pl.Element(PACK_T), pl.Element(k)),
                               lambda j, src8, split: (layer_idx, src8[j] * SUB, 0))],
        out_specs=pl.BlockSpec((PACK_T, k), lambda j, src8, split: (j, 0)))
    return pl.pallas_call(
        _pack_body, grid_spec=grid_spec,
        out_shape=jax.ShapeDtypeStruct((len(src) * PACK_T, k), bf16),
        compiler_params=_cparams("parallel"),
        name="pack_w_in",
    )(jnp.asarray([s // SUB for s, _ in src], jnp.int32), jnp.asarray([f for _, f in src], jnp.int32), w_in_t)


def layer(i, x, p, norm_pre, norm_post, w_in_t, conv_w, conv_b, dt_bias, a_log, d_skip, ssm_norm,
          cmp_pe, cmp_w1, cmp_w2, ret_norm, w_branch, w_out, ple_proj, ple_gate, ple_norm):
    src32, src16 = _pack_sources()
    w32, w16 = pack_w_in(w_in_t, i, src32), pack_w_in(w_in_t, i, src16)
    p32 = norm_matmul(x, norm_pre, w32, f32, "in_proj_f32")
    p16 = norm_matmul(x, norm_pre, w16, bf16, "in_proj_bf16")
    ya = ssd_mixer(p32, conv_w, conv_b, dt_bias, a_log, d_skip, ssm_norm)
    cmp = compress_kv(p32, cmp_pe, cmp_w1, cmp_w2)
    yb = nsa_mixer(p32, p16, cmp)
    yc = retention_mixer(p32, p16, ret_norm)
    merged = merge_branches(ya, yb, yc, w_branch, p32)
    return post_mixer(x, merged, p, w_out, ple_proj, ple_gate, norm_post, ple_norm)


def kernel(x, p, norm_pre, norm_post, w_in, conv_w, conv_b, dt_bias, a_log, d_skip, ssm_norm, cmp_pe, cmp_w1,
           cmp_w2, ret_norm, w_branch, w_out, ple_proj, ple_gate, ple_norm):
    b, L, d = x.shape
    assert b == 1 and d == D_MODEL
    xf = x.reshape(L, d)
    w_in_t = jnp.swapaxes(w_in, 1, 2)
    for i in range(p.shape[0]):
        xf = layer(i, xf, p[i, 0], norm_pre[i], norm_post[i], w_in_t, conv_w[i], conv_b[i], dt_bias[i], a_log[i],
                   d_skip[i], ssm_norm[i], cmp_pe[i], cmp_w1[i], cmp_w2[i], ret_norm[i], w_branch[i], w_out[i],
                   ple_proj[i], ple_gate[i], ple_norm[i])
    return xf.reshape(b, L, d)
```
